```python
import math
import jax, jax.numpy as jnp
from jax import lax
import numpy as np

D_MODEL = 1024
BATCH = 4
SEQ = 8192
DEPTH = 4

HEAD_DIM = 64
BLOCK = 128
SWA_Q_HEADS = 8
SWA_KV_HEADS = 2
SWA_WINDOW = 128
DIFF_HEADS = 4
DIFF_V_DIM = 2 * HEAD_DIM
BRANCH_WIDTH = SWA_Q_HEADS * HEAD_DIM
N_BRANCHES = 2
N_ALIBI_HEADS = SWA_Q_HEADS + DIFF_HEADS
FFN_HIDDEN = -(-8 * D_MODEL // (3 * 256)) * 256
COL_SWA_Q = SWA_Q_HEADS * HEAD_DIM
COL_SWA_K = SWA_KV_HEADS * HEAD_DIM
COL_SWA_V = SWA_KV_HEADS * HEAD_DIM
COL_DIFF_Q = DIFF_HEADS * 2 * HEAD_DIM
COL_DIFF_K = DIFF_HEADS * 2 * HEAD_DIM
COL_DIFF_V = DIFF_HEADS * DIFF_V_DIM
COL_GATE = N_BRANCHES * D_MODEL
IN_COLS = COL_SWA_Q + COL_SWA_K + COL_SWA_V + COL_DIFF_Q + COL_DIFF_K + COL_DIFF_V + COL_GATE
NEG = -1e30
EPS = 1e-6

kernel_name = "hybrid_swa_sink_diffattn_gated"


def rms_norm(x, gain):
    xf = x.astype(jnp.float32)
    y = xf * lax.rsqrt(jnp.mean(xf * xf, axis=-1, keepdims=True) + EPS)
    return (y * gain.astype(jnp.float32)).astype(x.dtype)


def alibi_slopes():
    return jnp.exp2(-8.0 * jnp.arange(1, N_ALIBI_HEADS + 1, dtype=jnp.float32) / N_ALIBI_HEADS)


def swa_attention(q, k, v, sinks, slopes):
    bsz, seq = q.shape[0], q.shape[1]
    nb = seq // BLOCK
    grp = SWA_Q_HEADS // SWA_KV_HEADS
    qb = q.reshape(bsz, nb, BLOCK, SWA_KV_HEADS, grp, HEAD_DIM)
    kb = k.reshape(bsz, nb, BLOCK, SWA_KV_HEADS, HEAD_DIM)
    vb = v.reshape(bsz, nb, BLOCK, SWA_KV_HEADS, HEAD_DIM)
    pad = ((0, 0), (1, 0), (0, 0), (0, 0), (0, 0))
    kcat = jnp.concatenate([jnp.pad(kb, pad)[:, :-1], kb], axis=2)
    vcat = jnp.concatenate([jnp.pad(vb, pad)[:, :-1], vb], axis=2)
    s = jnp.einsum("bnqhgd,bnkhd->bnhgqk", qb, kcat).astype(jnp.float32) * (HEAD_DIM ** -0.5)
    dist = (BLOCK + jnp.arange(BLOCK))[:, None] - jnp.arange(2 * BLOCK)[None, :]
    in_window = (dist >= 0) & (dist < SWA_WINDOW)
    has_prev = (jnp.arange(nb)[:, None] > 0) | (jnp.arange(2 * BLOCK)[None, :] >= BLOCK)
    mask = in_window[None] & has_prev[:, None, :]
    bias = -slopes.reshape(SWA_KV_HEADS, grp)[:, :, None, None] * dist.astype(jnp.float32)
    s = jnp.where(mask[None, :, None, None], s + bias, NEG)
    sink = jnp.broadcast_to(sinks.astype(jnp.float32).reshape(SWA_KV_HEADS, grp)[None, None, :, :, None, None],
                            s.shape[:-1] + (1,))
    p = jax.nn.softmax(jnp.concatenate([s, sink], axis=-1), axis=-1)[..., :-1]
    o = jnp.einsum("bnhgqk,bnkhd->bnqhgd", p.astype(v.dtype), vcat)
    return o.reshape(bsz, seq, SWA_Q_HEADS * HEAD_DIM)


def diff_attention(q, k, v, lam, slopes):
    bsz, seq = q.shape[0], q.shape[1]
    nb = seq // BLOCK
    qb = q.reshape(bsz, nb, BLOCK, DIFF_HEADS, 2, HEAD_DIM).transpose(1, 0, 2, 3, 4, 5)
    kpos = jnp.arange(seq)

    def one_block(args):
        q_blk, n = args
        s = jnp.einsum("bqhcd,bkhcd->bhcqk", q_blk, k).astype(jnp.float32) * (HEAD_DIM ** -0.5)
        dist = (n * BLOCK + jnp.arange(BLOCK))[:, None] - kpos[None, :]
        s = s - slopes[None, :, None, None, None] * dist.astype(jnp.float32)
        s = jnp.where(dist >= 0, s, NEG)
        p = jax.nn.softmax(s, axis=-1)
        w = p[:, :, 0] - lam * p[:, :, 1]
        return jnp.einsum("bhqk,bkhe->bqhe", w.astype(v.dtype), v)

    out = lax.map(one_block, (qb, jnp.arange(nb)))
    return out.transpose(1, 0, 2, 3, 4).reshape(bsz, seq, DIFF_HEADS, DIFF_V_DIM)


def setup_inputs(seed: int = 0) -> dict:
    key = jax.random.key(seed)
    ks = jax.random.split(key, 14)
    f32 = jnp.float32
    nrm = lambda k, shape, scale: jax.random.normal(k, shape, f32) * scale
    return {
        "x": jax.random.normal(ks[0], (BATCH, SEQ, D_MODEL), f32),
        "w_in": nrm(ks[1], (DEPTH, D_MODEL, IN_COLS), D_MODEL ** -0.5),
        "b_gate": nrm(ks[2], (DEPTH, N_BRANCHES, D_MODEL), 0.1),
        "w_branch": nrm(ks[3], (DEPTH, N_BRANCHES, BRANCH_WIDTH, D_MODEL), BRANCH_WIDTH ** -0.5),
        "w_o": nrm(ks[4], (DEPTH, D_MODEL, D_MODEL), D_MODEL ** -0.5),
        "norm_mix": 1.0 + nrm(ks[5], (DEPTH, D_MODEL), 0.02),
        "norm_ffn": 1.0 + nrm(ks[6], (DEPTH, D_MODEL), 0.02),
        "qk_norm_swa": 1.0 + nrm(ks[7], (DEPTH, 2, HEAD_DIM), 0.02),
        "qk_norm_diff": 1.0 + nrm(ks[8], (DEPTH, 2, HEAD_DIM), 0.02),
        "attn_sinks": nrm(ks[9], (DEPTH, SWA_Q_HEADS), 0.5),
        "diff_lambda": nrm(ks[10], (DEPTH, 4, HEAD_DIM), 0.1),
        "diff_subln": 1.0 + nrm(ks[11], (DEPTH, DIFF_V_DIM), 0.02),
        "w_ffn_in": nrm(ks[12], (DEPTH, D_MODEL, 2 * FFN_HIDDEN), D_MODEL ** -0.5),
        "w_ffn_out": nrm(ks[13], (DEPTH, FFN_HIDDEN, D_MODEL), FFN_HIDDEN ** -0.5),
    }


def reference(x, w_in, b_gate, w_branch, w_o, norm_mix, norm_ffn, qk_norm_swa, qk_norm_diff,
              attn_sinks, diff_lambda, diff_subln, w_ffn_in, w_ffn_out):
    bsz, seq = x.shape[0], x.shape[1]
    slopes = alibi_slopes()
    slopes_swa, slopes_diff = slopes[:SWA_Q_HEADS], slopes[SWA_Q_HEADS:]
    splits = list(np.cumsum([COL_SWA_Q, COL_SWA_K, COL_SWA_V, COL_DIFF_Q, COL_DIFF_K, COL_DIFF_V]))
    for l in range(DEPTH):
        h = rms_norm(x, norm_mix[l])
        proj = h @ w_in[l]
        qa, ka, va, qd, kd, vd, gate_logits = jnp.split(proj, splits, axis=-1)
        qa = rms_norm(qa.reshape(bsz, seq, SWA_Q_HEADS, HEAD_DIM), qk_norm_swa[l, 0])
        ka = rms_norm(ka.reshape(bsz, seq, SWA_KV_HEADS, HEAD_DIM), qk_norm_swa[l, 1])
        va = va.reshape(bsz, seq, SWA_KV_HEADS, HEAD_DIM)
        o_a = swa_attention(qa, ka, va, attn_sinks[l], slopes_swa)
        qd = rms_norm(qd.reshape(bsz, seq, DIFF_HEADS, 2, HEAD_DIM), qk_norm_diff[l, 0])
        kd = rms_norm(kd.reshape(bsz, seq, DIFF_HEADS, 2, HEAD_DIM), qk_norm_diff[l, 1])
        vd = vd.reshape(bsz, seq, DIFF_HEADS, DIFF_V_DIM)
        lam_init = 0.8 - 0.6 * math.exp(-0.3 * l)
        lp = diff_lambda[l].astype(jnp.float32)
        lam = jnp.exp(jnp.sum(lp[0] * lp[1])) - jnp.exp(jnp.sum(lp[2] * lp[3])) + lam_init
        o_d = diff_attention(qd, kd, vd, lam, slopes_diff)
        o_b = (rms_norm(o_d, diff_subln[l]) * (1.0 - lam_init)).reshape(bsz, seq, BRANCH_WIDTH)
        gates = jax.nn.sigmoid(gate_logits.reshape(bsz, seq, N_BRANCHES, D_MODEL) + b_gate[l])
        branches = jnp.stack([o_a, o_b], axis=2)
        up = jnp.einsum("bsnc,ncd->bsnd", branches, w_branch[l])
        merged = jnp.einsum("bsnd,bsnd->bsd", gates, up)
        x = x + merged @ w_o[l]
        h2 = rms_norm(x, norm_ffn[l])
        g, u = jnp.split(h2 @ w_ffn_in[l], 2, axis=-1)
        x = x + (jax.nn.silu(g) * u) @ w_ffn_out[l]
    return x
```

```python
import functools
import math

import jax
import jax.numpy as jnp
from jax import lax
from jax.experimental import pallas as pl
from jax.experimental.pallas import tpu as pltpu

F32 = jnp.float32
BF16 = jnp.bfloat16

HEAD_DIM = 64
BLOCK = 128
SWA_Q_HEADS = 8
SWA_KV_HEADS = 2
SWA_GROUP = SWA_Q_HEADS // SWA_KV_HEADS
DIFF_HEADS = 4
DIFF_V_DIM = 2 * HEAD_DIM
N_ALIBI_HEADS = SWA_Q_HEADS + DIFF_HEADS
NEG = -1e30
EPS = 1e-6
QK_SCALE = HEAD_DIM ** -0.5

V7X_VMEM_BYTES = 64 * 1024 * 1024
LANES = 128


def _vmem_limit(nbytes):
    return int(min(max(2 * nbytes, 32 * 1024 * 1024), V7X_VMEM_BYTES - 8 * 1024 * 1024))


def _dot(a, b):
    return jnp.dot(a, b, preferred_element_type=F32)


def _dot_nt(a, b):
    return lax.dot_general(a, b, (((1,), (1,)), ((), ())), preferred_element_type=F32)


def _group_mean(sq, bd):
    hi = sq.astype(BF16)
    lo = (sq - hi.astype(F32)).astype(BF16)
    return _dot(hi, bd) + _dot(lo, bd)


def _proj_kernel(x_ref, gmix_ref, w_ref, gcol_ref, bd_ref, bg_ref,
                 qa_ref, ka_ref, va_ref, qd_ref, kd_ref, vd_ref, gate_ref, h_ref):
    x = x_ref[...]
    ms = jnp.mean(x * x, axis=-1, keepdims=True)
    h_ref[...] = (x * lax.rsqrt(ms + EPS) * gmix_ref[...]).astype(BF16)

    def proj(c0, width):
        return _dot(h_ref[...], w_ref[:, c0:c0 + width])

    def normed(y, c0):
        width = y.shape[1]
        m = _group_mean(y * y, bd_ref[:width, :width])
        return y * lax.rsqrt(m + EPS) * gcol_ref[:, c0:c0 + width]

    for c in range(2):
        y = proj(256 * c, 256)
        qa_ref[:, 256 * c:256 * (c + 1)] = normed(y, 256 * c).astype(BF16)
    y = proj(512, 256)
    ka_ref[...] = normed(y[:, :128], 512).astype(BF16)
    va_ref[...] = y[:, 128:].astype(BF16)
    for c in range(2):
        y = normed(proj(768 + 256 * c, 256), 768 + 256 * c).astype(BF16)
        qd_ref[2 * c] = y[:, :128]
        qd_ref[2 * c + 1] = y[:, 128:]
    for c in range(2):
        y = normed(proj(1280 + 256 * c, 256), 1280 + 256 * c).astype(BF16)
        kd_ref[2 * c] = y[:, :128]
        kd_ref[2 * c + 1] = y[:, 128:]
    for c in range(2):
        y = proj(1792 + 256 * c, 256).astype(BF16)
        vd_ref[2 * c] = y[:, :128]
        vd_ref[2 * c + 1] = y[:, 128:]
    for c in range(8):
        y = proj(2304 + 256 * c, 256) + bg_ref[:, 256 * c:256 * (c + 1)]
        gate_ref[:, 256 * c:256 * (c + 1)] = jax.nn.sigmoid(y).astype(BF16)


def _proj(x, gmix, w, gcol, bd, bg, *, tm):
    t, d = x.shape
    ncols = w.shape[1]
    grid = (t // tm,)
    const = lambda i: (0, 0)
    out_shape = (
        jax.ShapeDtypeStruct((t, 512), BF16),
        jax.ShapeDtypeStruct((t, 128), BF16),
        jax.ShapeDtypeStruct((t, 128), BF16),
        jax.ShapeDtypeStruct((DIFF_HEADS, t, 128), BF16),
        jax.ShapeDtypeStruct((DIFF_HEADS, t, 128), BF16),
        jax.ShapeDtypeStruct((DIFF_HEADS, t, 128), BF16),
        jax.ShapeDtypeStruct((t, 2 * d), BF16),
    )
    row = lambda i: (i, 0)
    hrow = lambda i: (0, i, 0)
    out_specs = (
        pl.BlockSpec((tm, 512), row),
        pl.BlockSpec((tm, 128), row),
        pl.BlockSpec((tm, 128), row),
        pl.BlockSpec((DIFF_HEADS, tm, 128), hrow),
        pl.BlockSpec((DIFF_HEADS, tm, 128), hrow),
        pl.BlockSpec((DIFF_HEADS, tm, 128), hrow),
        pl.BlockSpec((tm, 2 * d), row),
    )
    in_specs = [
        pl.BlockSpec((tm, d), row),
        pl.BlockSpec((1, d), const),
        pl.BlockSpec((d, ncols), const),
        pl.BlockSpec((1, gcol.shape[1]), const),
        pl.BlockSpec(bd.shape, const),
        pl.BlockSpec((1, 2 * d), const),
    ]
    nbytes = 2 * (tm * d * 4 + d * ncols * 2 + tm * ncols * 2) + tm * d * 2
    return pl.pallas_call(
        _proj_kernel,
        grid=grid,
        in_specs=in_specs,
        out_specs=out_specs,
        out_shape=out_shape,
        scratch_shapes=[pltpu.VMEM((tm, d), BF16)],
        compiler_params=pltpu.CompilerParams(
            dimension_semantics=("arbitrary",), vmem_limit_bytes=_vmem_limit(nbytes)),
        name="proj",
    )(x, gmix, w, gcol, bd, bg)


def _swa_kernel(slopes_ref, sinks_ref, q_ref, k_ref, v_ref, o_ref, *, tq):
    i = pl.program_id(1)
    qi = lax.broadcasted_iota(jnp.int32, (BLOCK, BLOCK), 0)
    kj = lax.broadcasted_iota(jnp.int32, (BLOCK, BLOCK), 1)
    lower = kj <= qi
    dist = jnp.where(lower, qi - kj, BLOCK + qi - kj).astype(F32)
    for r in range(tq // BLOCK):
        n = i * (tq // BLOCK) + r
        cur = pl.multiple_of(n * BLOCK, BLOCK)
        prev = pl.multiple_of(jnp.maximum(n - 1, 0) * BLOCK, BLOCK)
        valid = jnp.logical_or(lower, n > 0)
        k_cur = k_ref[pl.ds(cur, BLOCK), :]
        k_prev = k_ref[pl.ds(prev, BLOCK), :]
        v_cur = v_ref[pl.ds(cur, BLOCK), :]
        v_prev = v_ref[pl.ds(prev, BLOCK), :]
        outs = []
        for hq in range(SWA_Q_HEADS):
            g = hq // SWA_GROUP
            gs = slice(g * HEAD_DIM, (g + 1) * HEAD_DIM)
            qh = q_ref[r * BLOCK:(r + 1) * BLOCK, hq * HEAD_DIM:(hq + 1) * HEAD_DIM]
            s_cur = _dot_nt(qh, k_cur[:, gs])
            s_prev = _dot_nt(qh, k_prev[:, gs])
            s = jnp.where(lower, s_cur, s_prev) - slopes_ref[hq] * dist
            s = jnp.where(valid, s, NEG)
            sink = sinks_ref[hq]
            m = jnp.maximum(jnp.max(s, axis=-1, keepdims=True), sink)
            e = jnp.exp(s - m)
            denom = jnp.sum(e, axis=-1, keepdims=True) + jnp.exp(sink - m)
            p = e / denom
            p_cur = jnp.where(lower, p, 0.0).astype(BF16)
            p_prev = jnp.where(lower, 0.0, p).astype(BF16)
            outs.append(_dot(p_cur, v_cur[:, gs]) + _dot(p_prev, v_prev[:, gs]))
        o_ref[r * BLOCK:(r + 1) * BLOCK, :] = jnp.concatenate(outs, axis=-1).astype(BF16)


def _swa(qa, ka, va, slopes, sinks, *, bsz, seq, tq):
    nq = seq // tq
    kernel = functools.partial(_swa_kernel, tq=tq)
    smem = pl.BlockSpec(memory_space=pltpu.SMEM)
    nbytes = 2 * (2 * tq * 512 * 2 + 2 * seq * 128 * 2)
    return pl.pallas_call(
        kernel,
        grid=(bsz, nq),
        in_specs=[
            smem, smem,
            pl.BlockSpec((tq, 512), lambda b, i: (b * nq + i, 0)),
            pl.BlockSpec((seq, 128), lambda b, i: (b, 0)),
            pl.BlockSpec((seq, 128), lambda b, i: (b, 0)),
        ],
        out_specs=pl.BlockSpec((tq, 512), lambda b, i: (b * nq + i, 0)),
        out_shape=jax.ShapeDtypeStruct(qa.shape, BF16),
        compiler_params=pltpu.CompilerParams(
            dimension_semantics=("arbitrary", "arbitrary"), vmem_limit_bytes=_vmem_limit(nbytes)),
        name="swa",
    )(slopes, sinks, qa, ka, va)


def _diff_kernel(slopes_ref, scal_ref, q_ref, k_ref, v_ref, lp_ref, g_ref, o_ref,
                 qs_ref, m_ref, l_ref, acc_ref, *, tq, tk):
    h = pl.program_id(1)
    i = pl.program_id(2)
    slope = slopes_ref[h]
    lam_init = scal_ref[0]

    q = q_ref[...]
    lane = lax.broadcasted_iota(jnp.int32, (tq, LANES), 1)
    zero = jnp.zeros_like(q)
    qs_ref[0:tq, :] = jnp.where(lane < HEAD_DIM, q, zero)
    qs_ref[tq:2 * tq, :] = jnp.where(lane < HEAD_DIM, zero, q)
    m_ref[...] = jnp.full(m_ref.shape, NEG, F32)
    l_ref[...] = jnp.zeros(l_ref.shape, F32)
    acc_ref[...] = jnp.zeros(acc_ref.shape, F32)

    def step(j, masked):
        start = pl.multiple_of(j * tk, tk)
        k = k_ref[pl.ds(start, tk), :]
        v = v_ref[pl.ds(start, tk), :]
        s = _dot_nt(qs_ref[...], k)
        col = start + lax.broadcasted_iota(jnp.int32, (1, tk), 1)
        s = s + slope * col.astype(F32)
        if masked:
            row = i * tq + lax.rem(lax.broadcasted_iota(jnp.int32, (2 * tq, 1), 0), tq)
            s = jnp.where(col <= row, s, NEG)
        m_prev = m_ref[...]
        m_new = jnp.maximum(m_prev, jnp.max(s, axis=-1, keepdims=True))
        alpha = jnp.exp(m_prev - m_new)
        p = jnp.exp(s - m_new)
        l_ref[...] = alpha * l_ref[...] + jnp.sum(p, axis=-1, keepdims=True)
        acc_ref[...] = alpha * acc_ref[...] + _dot(p.astype(BF16), v)
        m_ref[...] = m_new

    nfull = i * (tq // tk)

    def body(j, carry):
        step(j, False)
        return carry

    lax.fori_loop(0, nfull, body, 0)
    for d in range(tq // tk):
        step(nfull + d, True)

    lp = lp_ref[...]
    lam = (jnp.exp(jnp.sum(lp[0:1] * lp[1:2], axis=-1, keepdims=True))
           - jnp.exp(jnp.sum(lp[2:3] * lp[3:4], axis=-1, keepdims=True)) + lam_init)
    acc = acc_ref[...]
    l = l_ref[...]
    o = acc[0:tq] / l[0:tq] - lam * (acc[tq:2 * tq] / l[tq:2 * tq])
    ms = jnp.mean(o * o, axis=-1, keepdims=True)
    o_ref[...] = (o * lax.rsqrt(ms + EPS) * g_ref[...] * (1.0 - lam_init)).astype(BF16)


def _diff(qd, kd, vd, slopes, scal, lp, gsub, *, bsz, seq, tq, tk):
    nq = seq // tq
    t = bsz * seq
    kernel = functools.partial(_diff_kernel, tq=tq, tk=tk)
    smem = pl.BlockSpec(memory_space=pltpu.SMEM)
    nbytes = 2 * (2 * seq * 128 * 2 + 2 * tq * 128 * 2) + 2 * tq * (128 * 6 + 8) + 4 * 2 * tq * tk * 4
    return pl.pallas_call(
        kernel,
        grid=(bsz, DIFF_HEADS, nq),
        in_specs=[
            smem, smem,
            pl.BlockSpec((None, tq, 128), lambda b, h, i: (h, b * nq + i, 0)),
            pl.BlockSpec((None, seq, 128), lambda b, h, i: (h, b, 0)),
            pl.BlockSpec((None, seq, 128), lambda b, h, i: (h, b, 0)),
            pl.BlockSpec((4, HEAD_DIM), lambda b, h, i: (0, 0)),
            pl.BlockSpec((1, DIFF_V_DIM), lambda b, h, i: (0, 0)),
        ],
        out_specs=pl.BlockSpec((tq, 128), lambda b, h, i: (b * nq + i, h)),
        out_shape=jax.ShapeDtypeStruct((t, DIFF_HEADS * DIFF_V_DIM), BF16),
        scratch_shapes=[
            pltpu.VMEM((2 * tq, 128), BF16),
            pltpu.VMEM((2 * tq, 1), F32),
            pltpu.VMEM((2 * tq, 1), F32),
            pltpu.VMEM((2 * tq, 128), F32),
        ],
        compiler_params=pltpu.CompilerParams(
            dimension_semantics=("arbitrary", "arbitrary", "arbitrary"),
            vmem_limit_bytes=_vmem_limit(nbytes)),
        name="diff",
    )(slopes, scal, qd, kd, vd, lp, gsub)


def _merge_kernel(x_ref, oa_ref, ob_ref, gate_ref, wb_ref, wo_ref, out_ref):
    d = x_ref.shape[1]
    up_a = _dot(oa_ref[...], wb_ref[0])
    up_b = _dot(ob_ref[...], wb_ref[1])
    merged = gate_ref[:, :d].astype(F32) * up_a + gate_ref[:, d:].astype(F32) * up_b
    out_ref[...] = x_ref[...] + _dot(merged.astype(BF16), wo_ref[...])


def _merge(x, oa, ob, gates, wb, wo, *, tm):
    t, d = x.shape
    row = lambda i: (i, 0)
    nbytes = 2 * (2 * tm * d * 4 + 2 * tm * 512 * 2 + tm * 2 * d * 2 + 2 * 512 * d * 2 + d * d * 2)
    return pl.pallas_call(
        _merge_kernel,
        grid=(t // tm,),
        in_specs=[
            pl.BlockSpec((tm, d), row),
            pl.BlockSpec((tm, 512), row),
            pl.BlockSpec((tm, 512), row),
            pl.BlockSpec((tm, 2 * d), row),
            pl.BlockSpec(wb.shape, lambda i: (0, 0, 0)),
            pl.BlockSpec(wo.shape, lambda i: (0, 0)),
        ],
        out_specs=pl.BlockSpec((tm, d), row),
        out_shape=jax.ShapeDtypeStruct(x.shape, F32),
        compiler_params=pltpu.CompilerParams(
            dimension_semantics=("arbitrary",), vmem_limit_bytes=_vmem_limit(nbytes)),
        name="merge",
    )(x, oa, ob, gates, wb, wo)


def _ffn_kernel(x_ref, gn_ref, wg_ref, wu_ref, wout_ref, out_ref, h_ref, *, chunk):
    x = x_ref[...]
    ms = jnp.mean(x * x, axis=-1, keepdims=True)
    h_ref[...] = (x * lax.rsqrt(ms + EPS) * gn_ref[...]).astype(BF16)
    hidden = wg_ref.shape[1]
    acc = x
    for c in range(hidden // chunk):
        cs = slice(c * chunk, (c + 1) * chunk)
        g = _dot(h_ref[...], wg_ref[:, cs])
        u = _dot(h_ref[...], wu_ref[:, cs])
        a = (g * jax.nn.sigmoid(g) * u).astype(BF16)
        acc = acc + _dot(a, wout_ref[cs, :])
    out_ref[...] = acc


def _ffn(x, gn, wg, wu, wout, *, tm, chunk):
    t, d = x.shape
    hidden = wg.shape[1]
    row = lambda i: (i, 0)
    const = lambda i: (0, 0)
    kernel = functools.partial(_ffn_kernel, chunk=chunk)
    nbytes = 2 * (2 * tm * d * 4 + 3 * d * hidden * 2) + tm * d * 2 + 4 * tm * chunk * 4
    return pl.pallas_call(
        kernel,
        grid=(t // tm,),
        in_specs=[
            pl.BlockSpec((tm, d), row),
            pl.BlockSpec((1, d), const),
            pl.BlockSpec((d, hidden), const),
            pl.BlockSpec((d, hidden), const),
            pl.BlockSpec((hidden, d), const),
        ],
        out_specs=pl.BlockSpec((tm, d), row),
        out_shape=jax.ShapeDtypeStruct(x.shape, F32),
        scratch_shapes=[pltpu.VMEM((tm, d), BF16)],
        compiler_params=pltpu.CompilerParams(
            dimension_semantics=("arbitrary",), vmem_limit_bytes=_vmem_limit(nbytes)),
        name="ffn",
    )(x, gn, wg, wu, wout)


def _alibi_slopes():
    return jnp.exp2(-8.0 * jnp.arange(1, N_ALIBI_HEADS + 1, dtype=F32) / N_ALIBI_HEADS)


def _group_mean_matrix():
    idx = jnp.arange(256) // HEAD_DIM
    return jnp.where(idx[:, None] == idx[None, :], 1.0 / HEAD_DIM, 0.0).astype(BF16)


def kernel(x, w_in, b_gate, w_branch, w_o, norm_mix, norm_ffn, qk_norm_swa, qk_norm_diff,
           attn_sinks, diff_lambda, diff_subln, w_ffn_in, w_ffn_out):
    bsz, seq, d = x.shape
    depth = w_in.shape[0]
    hidden = w_ffn_out.shape[1]
    t = bsz * seq
    slopes = _alibi_slopes()
    slopes_swa, slopes_diff = slopes[:SWA_Q_HEADS], slopes[SWA_Q_HEADS:]
    bd = _group_mean_matrix()
    ones128 = jnp.ones((128,), F32)

    xf = x.reshape(t, d)
    for l in range(depth):
        gq_a = jnp.tile(qk_norm_swa[l, 0], SWA_Q_HEADS) * QK_SCALE
        gk_a = jnp.tile(qk_norm_swa[l, 1], SWA_KV_HEADS)
        gq_d = jnp.tile(qk_norm_diff[l, 0], 2 * DIFF_HEADS) * QK_SCALE
        gk_d = jnp.tile(qk_norm_diff[l, 1], 2 * DIFF_HEADS)
        gcol = jnp.concatenate([gq_a, gk_a, ones128, gq_d, gk_d]).reshape(1, -1).astype(F32)

        qa, ka, va, qd, kd, vd, gates = _proj(
            xf, norm_mix[l].reshape(1, d), w_in[l].astype(BF16), gcol, bd,
            b_gate[l].reshape(1, 2 * d), tm=512)

        o_a = _swa(qa, ka, va, slopes_swa, attn_sinks[l].astype(F32), bsz=bsz, seq=seq, tq=512)

        lam_init = 0.8 - 0.6 * math.exp(-0.3 * l)
        scal = jnp.array([lam_init], F32)
        o_b = _diff(qd, kd, vd, slopes_diff, scal, diff_lambda[l].astype(F32),
                    diff_subln[l].reshape(1, DIFF_V_DIM), bsz=bsz, seq=seq, tq=256, tk=256)

        xf = _merge(xf, o_a, o_b, gates, w_branch[l].astype(BF16), w_o[l].astype(BF16), tm=512)

        w1 = w_ffn_in[l].astype(BF16)
        xf = _ffn(xf, norm_ffn[l].reshape(1, d), w1[:, :hidden], w1[:, hidden:],
                  w_ffn_out[l].astype(BF16), tm=512, chunk=256)
    return xf.reshape(bsz, seq, d)
```

```python
import functools
import math

import jax
import jax.numpy as jnp
from jax import lax
from jax.experimental import pallas as pl
from jax.experimental.pallas import tpu as pltpu

F32 = jnp.float32
BF16 = jnp.bfloat16

HEAD_DIM = 64
BLOCK = 128
SWA_Q_HEADS = 8
SWA_KV_HEADS = 2
SWA_GROUP = SWA_Q_HEADS // SWA_KV_HEADS
DIFF_HEADS = 4
DIFF_V_DIM = 2 * HEAD_DIM
N_ALIBI_HEADS = SWA_Q_HEADS + DIFF_HEADS
NEG = -1e30
EPS = 1e-6
QK_SCALE = HEAD_DIM ** -0.5
LOG2E = math.log2(math.e)
SCORE_BOUND_PER_GAIN = HEAD_DIM * QK_SCALE
FIXED_REF_MAX_BOUND = 20.0

V7X_VMEM_BYTES = 64 * 1024 * 1024
LANES = 128


def _vmem_limit(nbytes):
    return int(min(max(2 * nbytes, 32 * 1024 * 1024), V7X_VMEM_BYTES - 8 * 1024 * 1024))


def _dot(a, b):
    return jnp.dot(a, b, preferred_element_type=F32)


def _dot_nt(a, b):
    return lax.dot_general(a, b, (((1,), (1,)), ((), ())), preferred_element_type=F32)


def _group_mean(sq, bd):
    hi = sq.astype(BF16)
    lo = (sq - hi.astype(F32)).astype(BF16)
    return _dot(hi, bd) + _dot(lo, bd)


def _proj_kernel(x_ref, gmix_ref, w_ref, gcol_ref, bd_ref, bg_ref,
                 qa_ref, ka_ref, va_ref, qd_ref, kd_ref, vd_ref, gate_ref, h_ref):
    x = x_ref[...]
    ms = jnp.mean(x * x, axis=-1, keepdims=True)
    h_ref[...] = (x * lax.rsqrt(ms + EPS) * gmix_ref[...]).astype(BF16)

    def proj(c0, width):
        return _dot(h_ref[...], w_ref[:, c0:c0 + width])

    def normed(y, c0):
        width = y.shape[1]
        m = _group_mean(y * y, bd_ref[:width, :width])
        return y * lax.rsqrt(m + EPS) * gcol_ref[:, c0:c0 + width]

    for c in range(2):
        y = proj(256 * c, 256)
        qa_ref[:, 256 * c:256 * (c + 1)] = normed(y, 256 * c).astype(BF16)
    y = proj(512, 256)
    ka_ref[...] = normed(y[:, :128], 512).astype(BF16)
    va_ref[...] = y[:, 128:].astype(BF16)
    for c in range(2):
        y = normed(proj(768 + 256 * c, 256), 768 + 256 * c).astype(BF16)
        qd_ref[2 * c] = y[:, :128]
        qd_ref[2 * c + 1] = y[:, 128:]
    for c in range(2):
        y = normed(proj(1280 + 256 * c, 256), 1280 + 256 * c).astype(BF16)
        kd_ref[2 * c] = y[:, :128]
        kd_ref[2 * c + 1] = y[:, 128:]
    for c in range(2):
        y = proj(1792 + 256 * c, 256).astype(BF16)
        vd_ref[2 * c] = y[:, :128]
        vd_ref[2 * c + 1] = y[:, 128:]
    for c in range(8):
        y = proj(2304 + 256 * c, 256) + bg_ref[:, 256 * c:256 * (c + 1)]
        gate_ref[:, 256 * c:256 * (c + 1)] = jax.nn.sigmoid(y).astype(BF16)


def _proj(x, gmix, w, gcol, bd, bg, *, tm):
    t, d = x.shape
    ncols = w.shape[1]
    grid = (t // tm,)
    const = lambda i: (0, 0)
    out_shape = (
        jax.ShapeDtypeStruct((t, 512), BF16),
        jax.ShapeDtypeStruct((t, 128), BF16),
        jax.ShapeDtypeStruct((t, 128), BF16),
        jax.ShapeDtypeStruct((DIFF_HEADS, t, 128), BF16),
        jax.ShapeDtypeStruct((DIFF_HEADS, t, 128), BF16),
        jax.ShapeDtypeStruct((DIFF_HEADS, t, 128), BF16),
        jax.ShapeDtypeStruct((t, 2 * d), BF16),
    )
    row = lambda i: (i, 0)
    hrow = lambda i: (0, i, 0)
    out_specs = (
        pl.BlockSpec((tm, 512), row),
        pl.BlockSpec((tm, 128), row),
        pl.BlockSpec((tm, 128), row),
        pl.BlockSpec((DIFF_HEADS, tm, 128), hrow),
        pl.BlockSpec((DIFF_HEADS, tm, 128), hrow),
        pl.BlockSpec((DIFF_HEADS, tm, 128), hrow),
        pl.BlockSpec((tm, 2 * d), row),
    )
    in_specs = [
        pl.BlockSpec((tm, d), row),
        pl.BlockSpec((1, d), const),
        pl.BlockSpec((d, ncols), const),
        pl.BlockSpec((1, gcol.shape[1]), const),
        pl.BlockSpec(bd.shape, const),
        pl.BlockSpec((1, 2 * d), const),
    ]
    nbytes = 2 * (tm * d * 4 + d * ncols * 2 + tm * ncols * 2) + tm * d * 2
    return pl.pallas_call(
        _proj_kernel,
        grid=grid,
        in_specs=in_specs,
        out_specs=out_specs,
        out_shape=out_shape,
        scratch_shapes=[pltpu.VMEM((tm, d), BF16)],
        compiler_params=pltpu.CompilerParams(
            dimension_semantics=("arbitrary",), vmem_limit_bytes=_vmem_limit(nbytes)),
        name="proj",
    )(x, gmix, w, gcol, bd, bg)


def _swa_kernel(slopes_ref, sinks_ref, q_ref, k_ref, v_ref, o_ref, *, tq):
    i = pl.program_id(1)
    qi = lax.broadcasted_iota(jnp.int32, (BLOCK, BLOCK), 0)
    kj = lax.broadcasted_iota(jnp.int32, (BLOCK, BLOCK), 1)
    lower = kj <= qi
    dist = jnp.where(lower, qi - kj, BLOCK + qi - kj).astype(F32)
    for r in range(tq // BLOCK):
        n = i * (tq // BLOCK) + r
        cur = pl.multiple_of(n * BLOCK, BLOCK)
        prev = pl.multiple_of(jnp.maximum(n - 1, 0) * BLOCK, BLOCK)
        valid = jnp.logical_or(lower, n > 0)
        k_cur = k_ref[pl.ds(cur, BLOCK), :]
        k_prev = k_ref[pl.ds(prev, BLOCK), :]
        v_cur = v_ref[pl.ds(cur, BLOCK), :]
        v_prev = v_ref[pl.ds(prev, BLOCK), :]
        outs = []
        for hq in range(SWA_Q_HEADS):
            g = hq // SWA_GROUP
            gs = slice(g * HEAD_DIM, (g + 1) * HEAD_DIM)
            qh = q_ref[r * BLOCK:(r + 1) * BLOCK, hq * HEAD_DIM:(hq + 1) * HEAD_DIM]
            s_cur = _dot_nt(qh, k_cur[:, gs])
            s_prev = _dot_nt(qh, k_prev[:, gs])
            s = jnp.where(lower, s_cur, s_prev) - slopes_ref[hq] * dist
            s = jnp.where(valid, s, NEG)
            sink = sinks_ref[hq]
            m = jnp.maximum(jnp.max(s, axis=-1, keepdims=True), sink)
            e = jnp.exp(s - m)
            denom = jnp.sum(e, axis=-1, keepdims=True) + jnp.exp(sink - m)
            p = e / denom
            p_cur = jnp.where(lower, p, 0.0).astype(BF16)
            p_prev = jnp.where(lower, 0.0, p).astype(BF16)
            outs.append(_dot(p_cur, v_cur[:, gs]) + _dot(p_prev, v_prev[:, gs]))
        o_ref[r * BLOCK:(r + 1) * BLOCK, :] = jnp.concatenate(outs, axis=-1).astype(BF16)


def _swa(qa, ka, va, slopes, sinks, *, bsz, seq, tq):
    nq = seq // tq
    kernel = functools.partial(_swa_kernel, tq=tq)
    smem = pl.BlockSpec(memory_space=pltpu.SMEM)
    nbytes = 2 * (2 * tq * 512 * 2 + 2 * seq * 128 * 2)
    return pl.pallas_call(
        kernel,
        grid=(bsz, nq),
        in_specs=[
            smem, smem,
            pl.BlockSpec((tq, 512), lambda b, i: (b * nq + i, 0)),
            pl.BlockSpec((seq, 128), lambda b, i: (b, 0)),
            pl.BlockSpec((seq, 128), lambda b, i: (b, 0)),
        ],
        out_specs=pl.BlockSpec((tq, 512), lambda b, i: (b * nq + i, 0)),
        out_shape=jax.ShapeDtypeStruct(qa.shape, BF16),
        compiler_params=pltpu.CompilerParams(
            dimension_semantics=("arbitrary", "arbitrary"), vmem_limit_bytes=_vmem_limit(nbytes)),
        name="swa",
    )(slopes, sinks, qa, ka, va)


def _stack_q(q, tq):
    lane = lax.broadcasted_iota(jnp.int32, (tq, LANES), 1)
    zero = jnp.zeros_like(q)
    return jnp.where(lane < HEAD_DIM, q, zero), jnp.where(lane < HEAD_DIM, zero, q)


def _diff_finish(o1, o2, lam_init, lp_ref, g_ref, o_ref):
    lp = lp_ref[...]
    lam = (jnp.exp(jnp.sum(lp[0:1] * lp[1:2], axis=-1, keepdims=True))
           - jnp.exp(jnp.sum(lp[2:3] * lp[3:4], axis=-1, keepdims=True)) + lam_init)
    o = o1 - lam * o2
    ms = jnp.mean(o * o, axis=-1, keepdims=True)
    o_ref[...] = (o * lax.rsqrt(ms + EPS) * g_ref[...] * (1.0 - lam_init)).astype(BF16)


def _causal_keep(i, tq, start, width):
    col = start + lax.broadcasted_iota(jnp.int32, (1, width), 1)
    row = i * tq + lax.rem(lax.broadcasted_iota(jnp.int32, (2 * tq, 1), 0), tq)
    return col <= row


def _aug_lanes(val, ones_first):
    hi = val.astype(BF16).astype(F32)
    r1 = val - hi
    mid = r1.astype(BF16).astype(F32)
    lo = (r1 - mid).astype(BF16).astype(F32)
    lane = lax.broadcasted_iota(jnp.int32, val.shape, 1)
    base = 3 if ones_first else 0
    ones_lo = 0 if ones_first else 3
    parts = jnp.where(lane == base, hi, jnp.where(lane == base + 1, mid,
                                                  jnp.where(lane == base + 2, lo, 0.0)))
    is_one = jnp.logical_and(lane >= ones_lo, lane < ones_lo + 3)
    return jnp.where(is_one, 1.0, parts).astype(BF16)


def _diff_fixed_kernel(scal_ref, q_ref, k_ref, v_ref, lp_ref, g_ref, o_ref,
                       kaug_ref, vaug_ref, qs_ref, acc_ref, p_ref, *, tq, seq, fill):
    h = pl.program_id(1)
    i = pl.program_id(2)
    lam_init = scal_ref[0]
    bound2 = scal_ref[1]
    slope2 = scal_ref[2 + h]

    @pl.when(i == 0)
    def _build_kv():
        def body(c, carry):
            r0 = pl.multiple_of(c * fill, fill)
            pos = (r0 + lax.broadcasted_iota(jnp.int32, (fill, LANES), 0)).astype(F32)
            kaug_ref[pl.ds(r0, fill), 0:LANES] = k_ref[pl.ds(r0, fill), :]
            kaug_ref[pl.ds(r0, fill), LANES:2 * LANES] = _aug_lanes(slope2 * pos, ones_first=False)
            vaug_ref[pl.ds(r0, fill), 0:LANES] = v_ref[pl.ds(r0, fill), :]
            vaug_ref[pl.ds(r0, fill), LANES:2 * LANES] = jnp.ones((fill, LANES), BF16)
            return carry
        lax.fori_loop(0, seq // fill, body, 0)

    q1, q2 = _stack_q(q_ref[...], tq)
    rowpos = (i * tq + lax.broadcasted_iota(jnp.int32, (tq, LANES), 0)).astype(F32)
    aug = _aug_lanes(-(slope2 * rowpos + bound2), ones_first=True)
    qs_ref[0:tq, 0:LANES] = q1
    qs_ref[tq:2 * tq, 0:LANES] = q2
    qs_ref[0:tq, LANES:2 * LANES] = aug
    qs_ref[tq:2 * tq, LANES:2 * LANES] = aug
    acc_ref[...] = jnp.zeros(acc_ref.shape, F32)

    def probs(j, masked):
        start = pl.multiple_of(j * tq, tq)
        p = jnp.exp2(_dot_nt(qs_ref[...], kaug_ref[pl.ds(start, tq), :]))
        if masked:
            p = jnp.where(_causal_keep(i, tq, start, tq), p, 0.0)
        p_ref[...] = p.astype(BF16)

    def accumulate(j):
        start = pl.multiple_of(j * tq, tq)
        acc_ref[...] += _dot(p_ref[...], vaug_ref[pl.ds(start, tq), :])

    probs(0, True)

    def body(j, carry):
        accumulate(j)
        probs(j + 1, False)
        return carry

    lax.fori_loop(0, jnp.maximum(i - 1, 0), body, 0)

    @pl.when(i > 0)
    def _diagonal():
        accumulate(i - 1)
        probs(i, True)

    accumulate(i)

    acc = acc_ref[...]
    o1 = acc[0:tq, 0:LANES] / acc[0:tq, LANES:2 * LANES]
    o2 = acc[tq:2 * tq, 0:LANES] / acc[tq:2 * tq, LANES:2 * LANES]
    _diff_finish(o1, o2, lam_init, lp_ref, g_ref, o_ref)


def _diff_online_kernel(scal_ref, q_ref, k_ref, v_ref, lp_ref, g_ref, o_ref,
                        qs_ref, m_ref, l_ref, acc_ref, *, tq):
    h = pl.program_id(1)
    i = pl.program_id(2)
    lam_init = scal_ref[0]
    slope2 = scal_ref[2 + h]

    q1, q2 = _stack_q(q_ref[...], tq)
    qs_ref[0:tq, :] = q1
    qs_ref[tq:2 * tq, :] = q2
    m_ref[...] = jnp.full(m_ref.shape, NEG, F32)
    l_ref[...] = jnp.zeros(l_ref.shape, F32)
    acc_ref[...] = jnp.zeros(acc_ref.shape, F32)

    def step(j, masked):
        start = pl.multiple_of(j * tq, tq)
        s = _dot_nt(qs_ref[...], k_ref[pl.ds(start, tq), :])
        col = start + lax.broadcasted_iota(jnp.int32, (1, tq), 1)
        s = s + slope2 * col.astype(F32)
        if masked:
            s = jnp.where(_causal_keep(i, tq, start, tq), s, NEG)
        m_prev = m_ref[...]
        m_new = jnp.maximum(m_prev, jnp.max(s, axis=-1, keepdims=True))
        alpha = jnp.exp2(m_prev - m_new)
        p = jnp.exp2(s - m_new)
        l_ref[...] = alpha * l_ref[...] + jnp.sum(p, axis=-1, keepdims=True)
        acc_ref[...] = alpha * acc_ref[...] + _dot(p.astype(BF16), v_ref[pl.ds(start, tq), :])
        m_ref[...] = m_new

    def body(j, carry):
        step(j, False)
        return carry

    lax.fori_loop(0, i, body, 0)
    step(i, True)

    acc = acc_ref[...]
    l = l_ref[...]
    _diff_finish(acc[0:tq] / l[0:tq], acc[tq:2 * tq] / l[tq:2 * tq], lam_init, lp_ref, g_ref, o_ref)


def _diff(qd, kd, vd, scal, lp, gsub, *, bsz, seq, tq, fixed_reference):
    nq = seq // tq
    t = bsz * seq
    if fixed_reference:
        kernel = functools.partial(_diff_fixed_kernel, tq=tq, seq=seq, fill=tq)
        scratch = [
            pltpu.VMEM((seq, 2 * LANES), BF16),
            pltpu.VMEM((seq, 2 * LANES), BF16),
            pltpu.VMEM((2 * tq, 2 * LANES), BF16),
            pltpu.VMEM((2 * tq, 2 * LANES), F32),
            pltpu.VMEM((2 * tq, tq), BF16),
        ]
        scratch_bytes = 2 * seq * 2 * LANES * 2 + 2 * tq * 2 * LANES * 6 + 2 * tq * tq * 2 + 3 * 2 * tq * tq * 4
        name = "diff"
    else:
        kernel = functools.partial(_diff_online_kernel, tq=tq)
        scratch = [
            pltpu.VMEM((2 * tq, LANES), BF16),
            pltpu.VMEM((2 * tq, 1), F32),
            pltpu.VMEM((2 * tq, 1), F32),
            pltpu.VMEM((2 * tq, LANES), F32),
        ]
        scratch_bytes = 2 * tq * (LANES * 6 + 8) + 6 * 2 * tq * tq * 4
        name = "diff_online"
    nbytes = 2 * (2 * seq * LANES * 2 + 2 * tq * LANES * 2) + scratch_bytes
    return pl.pallas_call(
        kernel,
        grid=(bsz, DIFF_HEADS, nq),
        in_specs=[
            pl.BlockSpec(memory_space=pltpu.SMEM),
            pl.BlockSpec((None, tq, LANES), lambda b, h, i: (h, b * nq + i, 0)),
            pl.BlockSpec((None, seq, LANES), lambda b, h, i: (h, b, 0)),
            pl.BlockSpec((None, seq, LANES), lambda b, h, i: (h, b, 0)),
            pl.BlockSpec((4, HEAD_DIM), lambda b, h, i: (0, 0)),
            pl.BlockSpec((1, DIFF_V_DIM), lambda b, h, i: (0, 0)),
        ],
        out_specs=pl.BlockSpec((tq, LANES), lambda b, h, i: (b * nq + i, h)),
        out_shape=jax.ShapeDtypeStruct((t, DIFF_HEADS * DIFF_V_DIM), BF16),
        scratch_shapes=scratch,
        compiler_params=pltpu.CompilerParams(
            dimension_semantics=("arbitrary", "arbitrary", "arbitrary"),
            vmem_limit_bytes=_vmem_limit(nbytes)),
        name=name,
    )(scal, qd, kd, vd, lp, gsub)


def _merge_kernel(x_ref, oa_ref, ob_ref, gate_ref, wb_ref, wo_ref, out_ref):
    d = x_ref.shape[1]
    up_a = _dot(oa_ref[...], wb_ref[0])
    up_b = _dot(ob_ref[...], wb_ref[1])
    merged = gate_ref[:, :d].astype(F32) * up_a + gate_ref[:, d:].astype(F32) * up_b
    out_ref[...] = x_ref[...] + _dot(merged.astype(BF16), wo_ref[...])


def _merge(x, oa, ob, gates, wb, wo, *, tm):
    t, d = x.shape
    row = lambda i: (i, 0)
    nbytes = 2 * (2 * tm * d * 4 + 2 * tm * 512 * 2 + tm * 2 * d * 2 + 2 * 512 * d * 2 + d * d * 2)
    return pl.pallas_call(
        _merge_kernel,
        grid=(t // tm,),
        in_specs=[
            pl.BlockSpec((tm, d), row),
            pl.BlockSpec((tm, 512), row),
            pl.BlockSpec((tm, 512), row),
            pl.BlockSpec((tm, 2 * d), row),
            pl.BlockSpec(wb.shape, lambda i: (0, 0, 0)),
            pl.BlockSpec(wo.shape, lambda i: (0, 0)),
        ],
        out_specs=pl.BlockSpec((tm, d), row),
        out_shape=jax.ShapeDtypeStruct(x.shape, F32),
        compiler_params=pltpu.CompilerParams(
            dimension_semantics=("arbitrary",), vmem_limit_bytes=_vmem_limit(nbytes)),
        name="merge",
    )(x, oa, ob, gates, wb, wo)


def _ffn_kernel(x_ref, gn_ref, wg_ref, wu_ref, wout_ref, out_ref, h_ref, *, chunk):
    x = x_ref[...]
    ms = jnp.mean(x * x, axis=-1, keepdims=True)
    h_ref[...] = (x * lax.rsqrt(ms + EPS) * gn_ref[...]).astype(BF16)
    hidden = wg_ref.shape[1]
    acc = x
    for c in range(hidden // chunk):
        cs = slice(c * chunk, (c + 1) * chunk)
        g = _dot(h_ref[...], wg_ref[:, cs])
        u = _dot(h_ref[...], wu_ref[:, cs])
        a = (g * jax.nn.sigmoid(g) * u).astype(BF16)
        acc = acc + _dot(a, wout_ref[cs, :])
    out_ref[...] = acc


def _ffn(x, gn, wg, wu, wout, *, tm, chunk):
    t, d = x.shape
    hidden = wg.shape[1]
    row = lambda i: (i, 0)
    const = lambda i: (0, 0)
    kernel = functools.partial(_ffn_kernel, chunk=chunk)
    nbytes = 2 * (2 * tm * d * 4 + 3 * d * hidden * 2) + tm * d * 2 + 4 * tm * chunk * 4
    return pl.pallas_call(
        kernel,
        grid=(t // tm,),
        in_specs=[
            pl.BlockSpec((tm, d), row),
            pl.BlockSpec((1, d), const),
            pl.BlockSpec((d, hidden), const),
            pl.BlockSpec((d, hidden), const),
            pl.BlockSpec((hidden, d), const),
        ],
        out_specs=pl.BlockSpec((tm, d), row),
        out_shape=jax.ShapeDtypeStruct(x.shape, F32),
        scratch_shapes=[pltpu.VMEM((tm, d), BF16)],
        compiler_params=pltpu.CompilerParams(
            dimension_semantics=("arbitrary",), vmem_limit_bytes=_vmem_limit(nbytes)),
        name="ffn",
    )(x, gn, wg, wu, wout)


def _alibi_slopes():
    return jnp.exp2(-8.0 * jnp.arange(1, N_ALIBI_HEADS + 1, dtype=F32) / N_ALIBI_HEADS)


def _group_mean_matrix():
    idx = jnp.arange(256) // HEAD_DIM
    return jnp.where(idx[:, None] == idx[None, :], 1.0 / HEAD_DIM, 0.0).astype(BF16)


def kernel(x, w_in, b_gate, w_branch, w_o, norm_mix, norm_ffn, qk_norm_swa, qk_norm_diff,
           attn_sinks, diff_lambda, diff_subln, w_ffn_in, w_ffn_out):
    bsz, seq, d = x.shape
    depth = w_in.shape[0]
    hidden = w_ffn_out.shape[1]
    t = bsz * seq
    slopes = _alibi_slopes()
    slopes_swa, slopes_diff = slopes[:SWA_Q_HEADS], slopes[SWA_Q_HEADS:]
    bd = _group_mean_matrix()
    ones128 = jnp.ones((128,), F32)

    xf = x.reshape(t, d)
    for l in range(depth):
        gq_a = jnp.tile(qk_norm_swa[l, 0], SWA_Q_HEADS) * QK_SCALE
        gk_a = jnp.tile(qk_norm_swa[l, 1], SWA_KV_HEADS)
        gq_d = jnp.tile(qk_norm_diff[l, 0], 2 * DIFF_HEADS) * (QK_SCALE * LOG2E)
        gk_d = jnp.tile(qk_norm_diff[l, 1], 2 * DIFF_HEADS)
        gcol = jnp.concatenate([gq_a, gk_a, ones128, gq_d, gk_d]).reshape(1, -1).astype(F32)

        qa, ka, va, qd, kd, vd, gates = _proj(
            xf, norm_mix[l].reshape(1, d), w_in[l].astype(BF16), gcol, bd,
            b_gate[l].reshape(1, 2 * d), tm=512)

        o_a = _swa(qa, ka, va, slopes_swa, attn_sinks[l].astype(F32), bsz=bsz, seq=seq, tq=512)

        lam_init = 0.8 - 0.6 * math.exp(-0.3 * l)
        bound = (SCORE_BOUND_PER_GAIN * jnp.max(jnp.abs(qk_norm_diff[l, 0]))
                 * jnp.max(jnp.abs(qk_norm_diff[l, 1]))).astype(F32)
        scal = jnp.concatenate([jnp.array([lam_init], F32), (bound * LOG2E).reshape(1),
                                slopes_diff * LOG2E]).astype(F32)
        diff_args = (qd, kd, vd, scal, diff_lambda[l].astype(F32), diff_subln[l].reshape(1, DIFF_V_DIM))
        o_b = lax.cond(
            bound <= FIXED_REF_MAX_BOUND,
            lambda a: _diff(*a, bsz=bsz, seq=seq, tq=512, fixed_reference=True),
            lambda a: _diff(*a, bsz=bsz, seq=seq, tq=256, fixed_reference=False),
            diff_args)

        xf = _merge(xf, o_a, o_b, gates, w_branch[l].astype(BF16), w_o[l].astype(BF16), tm=512)

        w1 = w_ffn_in[l].astype(BF16)
        xf = _ffn(xf, norm_ffn[l].reshape(1, d), w1[:, :hidden], w1[:, hidden:],
                  w_ffn_out[l].astype(BF16), tm=512, chunk=256)
    return xf.reshape(bsz, seq, d)
```

```python
import functools
import math

import jax
import jax.numpy as jnp
from jax import lax
from jax.experimental import pallas as pl
from jax.experimental.pallas import tpu as pltpu

F32 = jnp.float32
BF16 = jnp.bfloat16

HEAD_DIM = 64
BLOCK = 128
SWA_Q_HEADS = 8
SWA_KV_HEADS = 2
SWA_GROUP = SWA_Q_HEADS // SWA_KV_HEADS
DIFF_HEADS = 4
DIFF_V_DIM = 2 * HEAD_DIM
N_ALIBI_HEADS = SWA_Q_HEADS + DIFF_HEADS
NEG = -1e30
EPS = 1e-6
QK_SCALE = HEAD_DIM ** -0.5
LOG2E = math.log2(math.e)
SCORE_BOUND_PER_GAIN = HEAD_DIM * QK_SCALE
FIXED_REF_MAX_BOUND = 20.0

V7X_VMEM_BYTES = 64 * 1024 * 1024
LANES = 128


def _vmem_limit(nbytes):
    return int(min(max(2 * nbytes, 32 * 1024 * 1024), V7X_VMEM_BYTES - 8 * 1024 * 1024))


def _dot(a, b):
    return jnp.dot(a, b, preferred_element_type=F32)


def _dot_nt(a, b):
    return lax.dot_general(a, b, (((1,), (1,)), ((), ())), preferred_element_type=F32)


def _group_mean(sq, bd):
    hi = sq.astype(BF16)
    lo = (sq - hi.astype(F32)).astype(BF16)
    return _dot(hi, bd) + _dot(lo, bd)


def _proj_kernel(x_ref, gmix_ref, w_ref, gcol_ref, bd_ref, bg_ref,
                 qa_ref, ka_ref, va_ref, qd_ref, kd_ref, vd_ref, gate_ref, h_ref):
    x = x_ref[...]
    ms = jnp.mean(x * x, axis=-1, keepdims=True)
    h_ref[...] = (x * lax.rsqrt(ms + EPS) * gmix_ref[...]).astype(BF16)

    def proj(c0, width):
        return _dot(h_ref[...], w_ref[:, c0:c0 + width])

    def normed(y, c0):
        width = y.shape[1]
        m = _group_mean(y * y, bd_ref[:width, :width])
        return y * lax.rsqrt(m + EPS) * gcol_ref[:, c0:c0 + width]

    for c in range(2):
        y = proj(256 * c, 256)
        qa_ref[:, 256 * c:256 * (c + 1)] = normed(y, 256 * c).astype(BF16)
    y = proj(512, 256)
    ka_ref[...] = normed(y[:, :128], 512).astype(BF16)
    va_ref[...] = y[:, 128:].astype(BF16)
    for c in range(2):
        y = normed(proj(768 + 256 * c, 256), 768 + 256 * c).astype(BF16)
        qd_ref[2 * c] = y[:, :128]
        qd_ref[2 * c + 1] = y[:, 128:]
    for c in range(2):
        y = normed(proj(1280 + 256 * c, 256), 1280 + 256 * c).astype(BF16)
        kd_ref[2 * c] = y[:, :128]
        kd_ref[2 * c + 1] = y[:, 128:]
    for c in range(2):
        y = proj(1792 + 256 * c, 256).astype(BF16)
        vd_ref[2 * c] = y[:, :128]
        vd_ref[2 * c + 1] = y[:, 128:]
    for c in range(8):
        y = proj(2304 + 256 * c, 256) + bg_ref[:, 256 * c:256 * (c + 1)]
        gate_ref[:, 256 * c:256 * (c + 1)] = jax.nn.sigmoid(y).astype(BF16)


def _proj(x, gmix, w, gcol, bd, bg, *, tm):
    t, d = x.shape
    ncols = w.shape[1]
    grid = (t // tm,)
    const = lambda i: (0, 0)
    out_shape = (
        jax.ShapeDtypeStruct((t, 512), BF16),
        jax.ShapeDtypeStruct((t, 128), BF16),
        jax.ShapeDtypeStruct((t, 128), BF16),
        jax.ShapeDtypeStruct((DIFF_HEADS, t, 128), BF16),
        jax.ShapeDtypeStruct((DIFF_HEADS, t, 128), BF16),
        jax.ShapeDtypeStruct((DIFF_HEADS, t, 128), BF16),
        jax.ShapeDtypeStruct((t, 2 * d), BF16),
    )
    row = lambda i: (i, 0)
    hrow = lambda i: (0, i, 0)
    out_specs = (
        pl.BlockSpec((tm, 512), row),
        pl.BlockSpec((tm, 128), row),
        pl.BlockSpec((tm, 128), row),
        pl.BlockSpec((DIFF_HEADS, tm, 128), hrow),
        pl.BlockSpec((DIFF_HEADS, tm, 128), hrow),
        pl.BlockSpec((DIFF_HEADS, tm, 128), hrow),
        pl.BlockSpec((tm, 2 * d), row),
    )
    in_specs = [
        pl.BlockSpec((tm, d), row),
        pl.BlockSpec((1, d), const),
        pl.BlockSpec((d, ncols), const),
        pl.BlockSpec((1, gcol.shape[1]), const),
        pl.BlockSpec(bd.shape, const),
        pl.BlockSpec((1, 2 * d), const),
    ]
    nbytes = 2 * (tm * d * 4 + d * ncols * 2 + tm * ncols * 2) + tm * d * 2
    return pl.pallas_call(
        _proj_kernel,
        grid=grid,
        in_specs=in_specs,
        out_specs=out_specs,
        out_shape=out_shape,
        scratch_shapes=[pltpu.VMEM((tm, d), BF16)],
        compiler_params=pltpu.CompilerParams(
            dimension_semantics=("arbitrary",), vmem_limit_bytes=_vmem_limit(nbytes)),
        name="proj",
    )(x, gmix, w, gcol, bd, bg)


SWA_FILL = 512
MASKED_LOGIT_SHIFT = 1e30


def _swa_fixed_kernel(scal_ref, q_ref, k_ref, v_ref, o_ref, kbuf_ref, vbuf_ref, tab_ref,
                      *, tq, seq):
    b = pl.program_id(0)
    i = pl.program_id(1)
    nblk = tq // BLOCK
    rows = SWA_GROUP * BLOCK
    win = 2 * BLOCK

    @pl.when(jnp.logical_and(b == 0, i == 0))
    def _build_tables():
        row = lax.broadcasted_iota(jnp.int32, (rows, win), 0)
        col = lax.broadcasted_iota(jnp.int32, (rows, win), 1)
        head = lax.shift_right_logical(row, 7)
        dist = jnp.bitwise_and(row, BLOCK - 1) + BLOCK - col
        valid = jnp.logical_and(dist >= 0, dist < BLOCK)
        has_key = jnp.logical_and(valid, col >= BLOCK)
        distf = dist.astype(F32)
        for g in range(SWA_KV_HEADS):
            def per_head(base):
                h0 = SWA_GROUP * g
                return jnp.where(head == 0, scal_ref[base + h0],
                                 jnp.where(head == 1, scal_ref[base + h0 + 1],
                                           jnp.where(head == 2, scal_ref[base + h0 + 2],
                                                     scal_ref[base + h0 + 3])))
            t = per_head(0) * distf + per_head(2 * SWA_Q_HEADS)
            tab_ref[g] = jnp.where(valid, t, MASKED_LOGIT_SHIFT)
            tab_ref[SWA_KV_HEADS + g] = jnp.where(has_key, t, MASKED_LOGIT_SHIFT)

    @pl.when(i == 0)
    def _stage_kv():
        kbuf_ref[:, 0:BLOCK, :] = jnp.zeros((SWA_KV_HEADS, BLOCK, LANES), BF16)
        vbuf_ref[:, 0:BLOCK, :] = jnp.zeros((SWA_KV_HEADS, BLOCK, 2 * LANES), BF16)
        lane = lax.broadcasted_iota(jnp.int32, (SWA_FILL, LANES), 1)
        first = lane < HEAD_DIM

        def body(c, carry):
            r0 = pl.multiple_of(c * SWA_FILL, SWA_FILL)
            dst = pl.ds(pl.multiple_of(r0 + BLOCK, BLOCK), SWA_FILL)
            for src_ref, dst_ref in ((k_ref, kbuf_ref), (v_ref, vbuf_ref)):
                x = src_ref[pl.ds(r0, SWA_FILL), :].astype(F32)
                xr = pltpu.roll(x, HEAD_DIM, 1)
                dst_ref[0, dst, 0:LANES] = jnp.where(first, x, xr).astype(BF16)
                dst_ref[1, dst, 0:LANES] = jnp.where(first, xr, x).astype(BF16)
            vbuf_ref[:, dst, LANES:2 * LANES] = jnp.ones((SWA_KV_HEADS, SWA_FILL, LANES), BF16)
            return carry
        lax.fori_loop(0, seq // SWA_FILL, body, 0)

    lane = lax.broadcasted_iota(jnp.int32, (BLOCK, LANES), 1)
    first = lane < HEAD_DIM
    for r in range(nblk):
        n = i * nblk + r
        w0 = pl.multiple_of(n * BLOCK, BLOCK)
        tsel = jnp.where(n == 0, SWA_KV_HEADS, 0)
        rs = slice(r * BLOCK, (r + 1) * BLOCK)
        for g in range(SWA_KV_HEADS):
            slabs = [q_ref[rs, (2 * g + a) * LANES:(2 * g + a + 1) * LANES] for a in range(2)]
            zero = jnp.zeros_like(slabs[0])
            lhs = jnp.concatenate(
                [jnp.where(first, slabs[0], zero), jnp.where(first, zero, slabs[0]),
                 jnp.where(first, slabs[1], zero), jnp.where(first, zero, slabs[1])], axis=0)
            s = _dot_nt(lhs, kbuf_ref[g, pl.ds(w0, win), :])
            p = jnp.exp2(s - tab_ref[tsel + g]).astype(BF16)
            out = _dot(p, vbuf_ref[g, pl.ds(w0, win), :])
            for a in range(2):
                halves = []
                for k in range(2):
                    hq = SWA_GROUP * g + 2 * a + k
                    blk = out[(2 * a + k) * BLOCK:(2 * a + k + 1) * BLOCK]
                    denom = blk[:, LANES:2 * LANES] + scal_ref[3 * SWA_Q_HEADS + hq]
                    halves.append(blk[:, 0:LANES] / denom)
                o_ref[rs, (2 * g + a) * LANES:(2 * g + a + 1) * LANES] = (
                    jnp.where(first, halves[0], halves[1]).astype(BF16))


def _swa_online_kernel(scal_ref, q_ref, k_ref, v_ref, o_ref, *, tq):
    i = pl.program_id(1)
    qi = lax.broadcasted_iota(jnp.int32, (BLOCK, BLOCK), 0)
    kj = lax.broadcasted_iota(jnp.int32, (BLOCK, BLOCK), 1)
    lower = kj <= qi
    dist = jnp.where(lower, qi - kj, BLOCK + qi - kj).astype(F32)
    for r in range(tq // BLOCK):
        n = i * (tq // BLOCK) + r
        cur = pl.multiple_of(n * BLOCK, BLOCK)
        prev = pl.multiple_of(jnp.maximum(n - 1, 0) * BLOCK, BLOCK)
        valid = jnp.logical_or(lower, n > 0)
        k_cur = k_ref[pl.ds(cur, BLOCK), :]
        k_prev = k_ref[pl.ds(prev, BLOCK), :]
        v_cur = v_ref[pl.ds(cur, BLOCK), :]
        v_prev = v_ref[pl.ds(prev, BLOCK), :]
        outs = []
        for hq in range(SWA_Q_HEADS):
            g = hq // SWA_GROUP
            gs = slice(g * HEAD_DIM, (g + 1) * HEAD_DIM)
            qh = q_ref[r * BLOCK:(r + 1) * BLOCK, hq * HEAD_DIM:(hq + 1) * HEAD_DIM]
            s_cur = _dot_nt(qh, k_cur[:, gs])
            s_prev = _dot_nt(qh, k_prev[:, gs])
            s = jnp.where(lower, s_cur, s_prev) - scal_ref[hq] * dist
            s = jnp.where(valid, s, NEG)
            sink = scal_ref[SWA_Q_HEADS + hq]
            m = jnp.maximum(jnp.max(s, axis=-1, keepdims=True), sink)
            e = jnp.exp2(s - m)
            denom = jnp.sum(e, axis=-1, keepdims=True) + jnp.exp2(sink - m)
            p = e / denom
            p_cur = jnp.where(lower, p, 0.0).astype(BF16)
            p_prev = jnp.where(lower, 0.0, p).astype(BF16)
            outs.append(_dot(p_cur, v_cur[:, gs]) + _dot(p_prev, v_prev[:, gs]))
        o_ref[r * BLOCK:(r + 1) * BLOCK, :] = jnp.concatenate(outs, axis=-1).astype(BF16)


def _swa(qa, ka, va, scal, *, bsz, seq, tq, fixed_reference):
    nq = seq // tq
    nbytes = 2 * (2 * tq * 512 * 2 + 2 * seq * LANES * 2)
    if fixed_reference:
        kernel = functools.partial(_swa_fixed_kernel, tq=tq, seq=seq)
        rows = SWA_GROUP * BLOCK
        scratch = [
            pltpu.VMEM((SWA_KV_HEADS, seq + BLOCK, LANES), BF16),
            pltpu.VMEM((SWA_KV_HEADS, seq + BLOCK, 2 * LANES), BF16),
            pltpu.VMEM((2 * SWA_KV_HEADS, rows, 2 * BLOCK), F32),
        ]
        nbytes += (SWA_KV_HEADS * (seq + BLOCK) * 3 * LANES * 2 + 2 * SWA_KV_HEADS * rows * 2 * BLOCK * 4
                   + 8 * rows * 2 * BLOCK * 4)
        name = "swa"
    else:
        kernel = functools.partial(_swa_online_kernel, tq=tq)
        scratch = []
        name = "swa_online"
    return pl.pallas_call(
        kernel,
        grid=(bsz, nq),
        in_specs=[
            pl.BlockSpec(memory_space=pltpu.SMEM),
            pl.BlockSpec((tq, 512), lambda b, i: (b * nq + i, 0)),
            pl.BlockSpec((seq, LANES), lambda b, i: (b, 0)),
            pl.BlockSpec((seq, LANES), lambda b, i: (b, 0)),
        ],
        out_specs=pl.BlockSpec((tq, 512), lambda b, i: (b * nq + i, 0)),
        out_shape=jax.ShapeDtypeStruct(qa.shape, BF16),
        scratch_shapes=scratch,
        compiler_params=pltpu.CompilerParams(
            dimension_semantics=("arbitrary", "arbitrary"), vmem_limit_bytes=_vmem_limit(nbytes)),
        name=name,
    )(scal, qa, ka, va)


def _stack_q(q, tq):
    lane = lax.broadcasted_iota(jnp.int32, (tq, LANES), 1)
    zero = jnp.zeros_like(q)
    return jnp.where(lane < HEAD_DIM, q, zero), jnp.where(lane < HEAD_DIM, zero, q)


def _diff_finish(o1, o2, lam_init, lp_ref, g_ref, o_ref):
    lp = lp_ref[...]
    lam = (jnp.exp(jnp.sum(lp[0:1] * lp[1:2], axis=-1, keepdims=True))
           - jnp.exp(jnp.sum(lp[2:3] * lp[3:4], axis=-1, keepdims=True)) + lam_init)
    o = o1 - lam * o2
    ms = jnp.mean(o * o, axis=-1, keepdims=True)
    o_ref[...] = (o * lax.rsqrt(ms + EPS) * g_ref[...] * (1.0 - lam_init)).astype(BF16)


def _causal_keep(i, tq, start, width):
    col = start + lax.broadcasted_iota(jnp.int32, (1, width), 1)
    row = i * tq + lax.rem(lax.broadcasted_iota(jnp.int32, (2 * tq, 1), 0), tq)
    return col <= row


def _aug_lanes(val, ones_first):
    hi = val.astype(BF16).astype(F32)
    r1 = val - hi
    mid = r1.astype(BF16).astype(F32)
    lo = (r1 - mid).astype(BF16).astype(F32)
    lane = lax.broadcasted_iota(jnp.int32, val.shape, 1)
    base = 3 if ones_first else 0
    ones_lo = 0 if ones_first else 3
    parts = jnp.where(lane == base, hi, jnp.where(lane == base + 1, mid,
                                                  jnp.where(lane == base + 2, lo, 0.0)))
    is_one = jnp.logical_and(lane >= ones_lo, lane < ones_lo + 3)
    return jnp.where(is_one, 1.0, parts).astype(BF16)


def _diff_fixed_kernel(scal_ref, q_ref, k_ref, v_ref, lp_ref, g_ref, o_ref,
                       kaug_ref, vaug_ref, qs_ref, acc_ref, p_ref, *, tq, seq, fill):
    h = pl.program_id(1)
    i = pl.program_id(2)
    lam_init = scal_ref[0]
    bound2 = scal_ref[1]
    slope2 = scal_ref[2 + h]

    @pl.when(i == 0)
    def _build_kv():
        def body(c, carry):
            r0 = pl.multiple_of(c * fill, fill)
            pos = (r0 + lax.broadcasted_iota(jnp.int32, (fill, LANES), 0)).astype(F32)
            kaug_ref[pl.ds(r0, fill), 0:LANES] = k_ref[pl.ds(r0, fill), :]
            kaug_ref[pl.ds(r0, fill), LANES:2 * LANES] = _aug_lanes(slope2 * pos, ones_first=False)
            vaug_ref[pl.ds(r0, fill), 0:LANES] = v_ref[pl.ds(r0, fill), :]
            vaug_ref[pl.ds(r0, fill), LANES:2 * LANES] = jnp.ones((fill, LANES), BF16)
            return carry
        lax.fori_loop(0, seq // fill, body, 0)

    q1, q2 = _stack_q(q_ref[...], tq)
    rowpos = (i * tq + lax.broadcasted_iota(jnp.int32, (tq, LANES), 0)).astype(F32)
    aug = _aug_lanes(-(slope2 * rowpos + bound2), ones_first=True)
    qs_ref[0:tq, 0:LANES] = q1
    qs_ref[tq:2 * tq, 0:LANES] = q2
    qs_ref[0:tq, LANES:2 * LANES] = aug
    qs_ref[tq:2 * tq, LANES:2 * LANES] = aug
    acc_ref[...] = jnp.zeros(acc_ref.shape, F32)

    def probs(j, masked):
        start = pl.multiple_of(j * tq, tq)
        p = jnp.exp2(_dot_nt(qs_ref[...], kaug_ref[pl.ds(start, tq), :]))
        if masked:
            p = jnp.where(_causal_keep(i, tq, start, tq), p, 0.0)
        p_ref[...] = p.astype(BF16)

    def accumulate(j):
        start = pl.multiple_of(j * tq, tq)
        acc_ref[...] += _dot(p_ref[...], vaug_ref[pl.ds(start, tq), :])

    probs(0, True)

    def body(j, carry):
        accumulate(j)
        probs(j + 1, False)
        return carry

    lax.fori_loop(0, jnp.maximum(i - 1, 0), body, 0)

    @pl.when(i > 0)
    def _diagonal():
        accumulate(i - 1)
        probs(i, True)

    accumulate(i)

    acc = acc_ref[...]
    o1 = acc[0:tq, 0:LANES] / acc[0:tq, LANES:2 * LANES]
    o2 = acc[tq:2 * tq, 0:LANES] / acc[tq:2 * tq, LANES:2 * LANES]
    _diff_finish(o1, o2, lam_init, lp_ref, g_ref, o_ref)


def _diff_online_kernel(scal_ref, q_ref, k_ref, v_ref, lp_ref, g_ref, o_ref,
                        qs_ref, m_ref, l_ref, acc_ref, *, tq):
    h = pl.program_id(1)
    i = pl.program_id(2)
    lam_init = scal_ref[0]
    slope2 = scal_ref[2 + h]

    q1, q2 = _stack_q(q_ref[...], tq)
    qs_ref[0:tq, :] = q1
    qs_ref[tq:2 * tq, :] = q2
    m_ref[...] = jnp.full(m_ref.shape, NEG, F32)
    l_ref[...] = jnp.zeros(l_ref.shape, F32)
    acc_ref[...] = jnp.zeros(acc_ref.shape, F32)

    def step(j, masked):
        start = pl.multiple_of(j * tq, tq)
        s = _dot_nt(qs_ref[...], k_ref[pl.ds(start, tq), :])
        col = start + lax.broadcasted_iota(jnp.int32, (1, tq), 1)
        s = s + slope2 * col.astype(F32)
        if masked:
            s = jnp.where(_causal_keep(i, tq, start, tq), s, NEG)
        m_prev = m_ref[...]
        m_new = jnp.maximum(m_prev, jnp.max(s, axis=-1, keepdims=True))
        alpha = jnp.exp2(m_prev - m_new)
        p = jnp.exp2(s - m_new)
        l_ref[...] = alpha * l_ref[...] + jnp.sum(p, axis=-1, keepdims=True)
        acc_ref[...] = alpha * acc_ref[...] + _dot(p.astype(BF16), v_ref[pl.ds(start, tq), :])
        m_ref[...] = m_new

    def body(j, carry):
        step(j, False)
        return carry

    lax.fori_loop(0, i, body, 0)
    step(i, True)

    acc = acc_ref[...]
    l = l_ref[...]
    _diff_finish(acc[0:tq] / l[0:tq], acc[tq:2 * tq] / l[tq:2 * tq], lam_init, lp_ref, g_ref, o_ref)


def _diff(qd, kd, vd, scal, lp, gsub, *, bsz, seq, tq, fixed_reference):
    nq = seq // tq
    t = bsz * seq
    if fixed_reference:
        kernel = functools.partial(_diff_fixed_kernel, tq=tq, seq=seq, fill=tq)
        scratch = [
            pltpu.VMEM((seq, 2 * LANES), BF16),
            pltpu.VMEM((seq, 2 * LANES), BF16),
            pltpu.VMEM((2 * tq, 2 * LANES), BF16),
            pltpu.VMEM((2 * tq, 2 * LANES), F32),
            pltpu.VMEM((2 * tq, tq), BF16),
        ]
        scratch_bytes = 2 * seq * 2 * LANES * 2 + 2 * tq * 2 * LANES * 6 + 2 * tq * tq * 2 + 3 * 2 * tq * tq * 4
        name = "diff"
    else:
        kernel = functools.partial(_diff_online_kernel, tq=tq)
        scratch = [
            pltpu.VMEM((2 * tq, LANES), BF16),
            pltpu.VMEM((2 * tq, 1), F32),
            pltpu.VMEM((2 * tq, 1), F32),
            pltpu.VMEM((2 * tq, LANES), F32),
        ]
        scratch_bytes = 2 * tq * (LANES * 6 + 8) + 6 * 2 * tq * tq * 4
        name = "diff_online"
    nbytes = 2 * (2 * seq * LANES * 2 + 2 * tq * LANES * 2) + scratch_bytes
    return pl.pallas_call(
        kernel,
        grid=(bsz, DIFF_HEADS, nq),
        in_specs=[
            pl.BlockSpec(memory_space=pltpu.SMEM),
            pl.BlockSpec((None, tq, LANES), lambda b, h, i: (h, b * nq + i, 0)),
            pl.BlockSpec((None, seq, LANES), lambda b, h, i: (h, b, 0)),
            pl.BlockSpec((None, seq, LANES), lambda b, h, i: (h, b, 0)),
            pl.BlockSpec((4, HEAD_DIM), lambda b, h, i: (0, 0)),
            pl.BlockSpec((1, DIFF_V_DIM), lambda b, h, i: (0, 0)),
        ],
        out_specs=pl.BlockSpec((tq, LANES), lambda b, h, i: (b * nq + i, h)),
        out_shape=jax.ShapeDtypeStruct((t, DIFF_HEADS * DIFF_V_DIM), BF16),
        scratch_shapes=scratch,
        compiler_params=pltpu.CompilerParams(
            dimension_semantics=("arbitrary", "arbitrary", "arbitrary"),
            vmem_limit_bytes=_vmem_limit(nbytes)),
        name=name,
    )(scal, qd, kd, vd, lp, gsub)


def _merge_kernel(x_ref, oa_ref, ob_ref, gate_ref, wb_ref, wo_ref, out_ref):
    d = x_ref.shape[1]
    up_a = _dot(oa_ref[...], wb_ref[0])
    up_b = _dot(ob_ref[...], wb_ref[1])
    merged = gate_ref[:, :d].astype(F32) * up_a + gate_ref[:, d:].astype(F32) * up_b
    out_ref[...] = x_ref[...] + _dot(merged.astype(BF16), wo_ref[...])


def _merge(x, oa, ob, gates, wb, wo, *, tm):
    t, d = x.shape
    row = lambda i: (i, 0)
    nbytes = 2 * (2 * tm * d * 4 + 2 * tm * 512 * 2 + tm * 2 * d * 2 + 2 * 512 * d * 2 + d * d * 2)
    return pl.pallas_call(
        _merge_kernel,
        grid=(t // tm,),
        in_specs=[
            pl.BlockSpec((tm, d), row),
            pl.BlockSpec((tm, 512), row),
            pl.BlockSpec((tm, 512), row),
            pl.BlockSpec((tm, 2 * d), row),
            pl.BlockSpec(wb.shape, lambda i: (0, 0, 0)),
            pl.BlockSpec(wo.shape, lambda i: (0, 0)),
        ],
        out_specs=pl.BlockSpec((tm, d), row),
        out_shape=jax.ShapeDtypeStruct(x.shape, F32),
        compiler_params=pltpu.CompilerParams(
            dimension_semantics=("arbitrary",), vmem_limit_bytes=_vmem_limit(nbytes)),
        name="merge",
    )(x, oa, ob, gates, wb, wo)


def _ffn_kernel(x_ref, gn_ref, wg_ref, wu_ref, wout_ref, out_ref, h_ref, *, chunk):
    x = x_ref[...]
    ms = jnp.mean(x * x, axis=-1, keepdims=True)
    h_ref[...] = (x * lax.rsqrt(ms + EPS) * gn_ref[...]).astype(BF16)
    hidden = wg_ref.shape[1]
    acc = x
    for c in range(hidden // chunk):
        cs = slice(c * chunk, (c + 1) * chunk)
        g = _dot(h_ref[...], wg_ref[:, cs])
        u = _dot(h_ref[...], wu_ref[:, cs])
        a = (g * jax.nn.sigmoid(g) * u).astype(BF16)
        acc = acc + _dot(a, wout_ref[cs, :])
    out_ref[...] = acc


def _ffn(x, gn, wg, wu, wout, *, tm, chunk):
    t, d = x.shape
    hidden = wg.shape[1]
    row = lambda i: (i, 0)
    const = lambda i: (0, 0)
    kernel = functools.partial(_ffn_kernel, chunk=chunk)
    nbytes = 2 * (2 * tm * d * 4 + 3 * d * hidden * 2) + tm * d * 2 + 4 * tm * chunk * 4
    return pl.pallas_call(
        kernel,
        grid=(t // tm,),
        in_specs=[
            pl.BlockSpec((tm, d), row),
            pl.BlockSpec((1, d), const),
            pl.BlockSpec((d, hidden), const),
            pl.BlockSpec((d, hidden), const),
            pl.BlockSpec((hidden, d), const),
        ],
        out_specs=pl.BlockSpec((tm, d), row),
        out_shape=jax.ShapeDtypeStruct(x.shape, F32),
        scratch_shapes=[pltpu.VMEM((tm, d), BF16)],
        compiler_params=pltpu.CompilerParams(
            dimension_semantics=("arbitrary",), vmem_limit_bytes=_vmem_limit(nbytes)),
        name="ffn",
    )(x, gn, wg, wu, wout)


def _alibi_slopes():
    return jnp.exp2(-8.0 * jnp.arange(1, N_ALIBI_HEADS + 1, dtype=F32) / N_ALIBI_HEADS)


def _group_mean_matrix():
    idx = jnp.arange(256) // HEAD_DIM
    return jnp.where(idx[:, None] == idx[None, :], 1.0 / HEAD_DIM, 0.0).astype(BF16)


def kernel(x, w_in, b_gate, w_branch, w_o, norm_mix, norm_ffn, qk_norm_swa, qk_norm_diff,
           attn_sinks, diff_lambda, diff_subln, w_ffn_in, w_ffn_out):
    bsz, seq, d = x.shape
    depth = w_in.shape[0]
    hidden = w_ffn_out.shape[1]
    t = bsz * seq
    slopes = _alibi_slopes()
    slopes_swa, slopes_diff = slopes[:SWA_Q_HEADS], slopes[SWA_Q_HEADS:]
    bd = _group_mean_matrix()
    ones128 = jnp.ones((128,), F32)

    xf = x.reshape(t, d)
    for l in range(depth):
        gq_a = jnp.tile(qk_norm_swa[l, 0], SWA_Q_HEADS) * (QK_SCALE * LOG2E)
        gk_a = jnp.tile(qk_norm_swa[l, 1], SWA_KV_HEADS)
        gq_d = jnp.tile(qk_norm_diff[l, 0], 2 * DIFF_HEADS) * (QK_SCALE * LOG2E)
        gk_d = jnp.tile(qk_norm_diff[l, 1], 2 * DIFF_HEADS)
        gcol = jnp.concatenate([gq_a, gk_a, ones128, gq_d, gk_d]).reshape(1, -1).astype(F32)

        qa, ka, va, qd, kd, vd, gates = _proj(
            xf, norm_mix[l].reshape(1, d), w_in[l].astype(BF16), gcol, bd,
            b_gate[l].reshape(1, 2 * d), tm=512)

        bound_a = (SCORE_BOUND_PER_GAIN * jnp.max(jnp.abs(qk_norm_swa[l, 0]))
                   * jnp.max(jnp.abs(qk_norm_swa[l, 1]))).astype(F32)
        sinks2 = attn_sinks[l].astype(F32) * LOG2E
        ref_a = jnp.maximum(bound_a * LOG2E, sinks2)
        scal_a = jnp.concatenate([slopes_swa * LOG2E, sinks2, ref_a, jnp.exp2(sinks2 - ref_a)]).astype(F32)
        o_a = lax.cond(
            bound_a <= FIXED_REF_MAX_BOUND,
            lambda a: _swa(*a, bsz=bsz, seq=seq, tq=512, fixed_reference=True),
            lambda a: _swa(*a, bsz=bsz, seq=seq, tq=512, fixed_reference=False),
            (qa, ka, va, scal_a))

        lam_init = 0.8 - 0.6 * math.exp(-0.3 * l)
        bound = (SCORE_BOUND_PER_GAIN * jnp.max(jnp.abs(qk_norm_diff[l, 0]))
                 * jnp.max(jnp.abs(qk_norm_diff[l, 1]))).astype(F32)
        scal = jnp.concatenate([jnp.array([lam_init], F32), (bound * LOG2E).reshape(1),
                                slopes_diff * LOG2E]).astype(F32)
        diff_args = (qd, kd, vd, scal, diff_lambda[l].astype(F32), diff_subln[l].reshape(1, DIFF_V_DIM))
        o_b = lax.cond(
            bound <= FIXED_REF_MAX_BOUND,
            lambda a: _diff(*a, bsz=bsz, seq=seq, tq=512, fixed_reference=True),
            lambda a: _diff(*a, bsz=bsz, seq=seq, tq=256, fixed_reference=False),
            diff_args)

        xf = _merge(xf, o_a, o_b, gates, w_branch[l].astype(BF16), w_o[l].astype(BF16), tm=512)

        w1 = w_ffn_in[l].astype(BF16)
        xf = _ffn(xf, norm_ffn[l].reshape(1, d), w1[:, :hidden], w1[:, hidden:],
                  w_ffn_out[l].astype(BF16), tm=512, chunk=256)
    return xf.reshape(bsz, seq, d)
```

```python
import functools
import math

import jax
import jax.numpy as jnp
from jax import lax
from jax.experimental import pallas as pl
from jax.experimental.pallas import tpu as pltpu

F32 = jnp.float32
BF16 = jnp.bfloat16

HEAD_DIM = 64
BLOCK = 128
SWA_Q_HEADS = 8
SWA_KV_HEADS = 2
SWA_GROUP = SWA_Q_HEADS // SWA_KV_HEADS
DIFF_HEADS = 4
DIFF_V_DIM = 2 * HEAD_DIM
N_ALIBI_HEADS = SWA_Q_HEADS + DIFF_HEADS
NEG = -1e30
EPS = 1e-6
QK_SCALE = HEAD_DIM ** -0.5
LOG2E = math.log2(math.e)
SCORE_BOUND_PER_GAIN = HEAD_DIM * QK_SCALE
FIXED_REF_MAX_BOUND = 20.0

V7X_VMEM_BYTES = 64 * 1024 * 1024
LANES = 128


def _vmem_limit(nbytes):
    return int(min(max(2 * nbytes, 32 * 1024 * 1024), V7X_VMEM_BYTES - 8 * 1024 * 1024))


def _dot(a, b):
    return jnp.dot(a, b, preferred_element_type=F32)


def _dot_nt(a, b):
    return lax.dot_general(a, b, (((1,), (1,)), ((), ())), preferred_element_type=F32)


PROJ_CHUNK = 256
PROJ_NORMED = 1792


def _proj_kernel(x_ref, gmix_ref, w_ref, gcol_ref, bd_ref, bg_ref,
                 qa_ref, ka_ref, va_ref, qd_ref, kd_ref, vd_ref, gate_ref, h_ref, y_ref):
    tm = x_ref.shape[0]
    x = x_ref[...]
    ms = jnp.mean(x * x, axis=-1, keepdims=True)
    h_ref[...] = (x * lax.rsqrt(ms + EPS) * gmix_ref[...]).astype(BF16)

    def proj(c0, width):
        return _dot(h_ref[...], w_ref[:, c0:c0 + width])

    nchunk = PROJ_NORMED // PROJ_CHUNK
    y_ref[...] = proj(0, PROJ_NORMED)
    sq = jnp.concatenate(
        [jnp.square(y_ref[:, PROJ_CHUNK * c:PROJ_CHUNK * (c + 1)]).astype(BF16) for c in range(nchunk)],
        axis=0)
    inv = lax.rsqrt(_dot(sq, bd_ref[...]) + EPS)

    def normed(c):
        cs = slice(PROJ_CHUNK * c, PROJ_CHUNK * (c + 1))
        return (y_ref[:, cs] * inv[c * tm:(c + 1) * tm] * gcol_ref[:, cs]).astype(BF16)

    for c in range(2):
        qa_ref[:, PROJ_CHUNK * c:PROJ_CHUNK * (c + 1)] = normed(c)
    for c in range(2):
        y = normed(2 + c)
        qd_ref[2 * c] = y[:, :LANES]
        qd_ref[2 * c + 1] = y[:, LANES:]
    for c in range(2):
        y = normed(4 + c)
        kd_ref[2 * c] = y[:, :LANES]
        kd_ref[2 * c + 1] = y[:, LANES:]
    ka_ref[...] = normed(6)[:, :LANES]
    va_ref[...] = y_ref[:, PROJ_NORMED - LANES:PROJ_NORMED].astype(BF16)
    for c in range(2):
        y = proj(PROJ_NORMED + PROJ_CHUNK * c, PROJ_CHUNK).astype(BF16)
        vd_ref[2 * c] = y[:, :LANES]
        vd_ref[2 * c + 1] = y[:, LANES:]
    g0 = PROJ_NORMED + 2 * PROJ_CHUNK
    for c in range(gate_ref.shape[1] // PROJ_CHUNK):
        cs = slice(PROJ_CHUNK * c, PROJ_CHUNK * (c + 1))
        y = proj(g0 + PROJ_CHUNK * c, PROJ_CHUNK) + bg_ref[:, cs]
        gate_ref[:, cs] = jax.nn.sigmoid(y).astype(BF16)


def _proj(x, gmix, w, gcol, bd, bg, *, tm):
    t, d = x.shape
    ncols = w.shape[1]
    grid = (t // tm,)
    const = lambda i: (0, 0)
    out_shape = (
        jax.ShapeDtypeStruct((t, 512), BF16),
        jax.ShapeDtypeStruct((t, 128), BF16),
        jax.ShapeDtypeStruct((t, 128), BF16),
        jax.ShapeDtypeStruct((DIFF_HEADS, t, 128), BF16),
        jax.ShapeDtypeStruct((DIFF_HEADS, t, 128), BF16),
        jax.ShapeDtypeStruct((DIFF_HEADS, t, 128), BF16),
        jax.ShapeDtypeStruct((t, 2 * d), BF16),
    )
    row = lambda i: (i, 0)
    hrow = lambda i: (0, i, 0)
    out_specs = (
        pl.BlockSpec((tm, 512), row),
        pl.BlockSpec((tm, 128), row),
        pl.BlockSpec((tm, 128), row),
        pl.BlockSpec((DIFF_HEADS, tm, 128), hrow),
        pl.BlockSpec((DIFF_HEADS, tm, 128), hrow),
        pl.BlockSpec((DIFF_HEADS, tm, 128), hrow),
        pl.BlockSpec((tm, 2 * d), row),
    )
    in_specs = [
        pl.BlockSpec((tm, d), row),
        pl.BlockSpec((1, d), const),
        pl.BlockSpec((d, ncols), const),
        pl.BlockSpec((1, gcol.shape[1]), const),
        pl.BlockSpec(bd.shape, const),
        pl.BlockSpec((1, 2 * d), const),
    ]
    nbytes = 2 * (tm * d * 4 + d * ncols * 2 + tm * ncols * 2) + tm * d * 2 + 3 * tm * PROJ_NORMED * 4
    return pl.pallas_call(
        _proj_kernel,
        grid=grid,
        in_specs=in_specs,
        out_specs=out_specs,
        out_shape=out_shape,
        scratch_shapes=[pltpu.VMEM((tm, d), BF16), pltpu.VMEM((tm, PROJ_NORMED), F32)],
        compiler_params=pltpu.CompilerParams(
            dimension_semantics=("arbitrary",), vmem_limit_bytes=_vmem_limit(nbytes)),
        name="proj",
    )(x, gmix, w, gcol, bd, bg)


SWA_FILL = 512
MASKED_LOGIT_SHIFT = 1e30


def _swa_fixed_kernel(scal_ref, q_ref, k_ref, v_ref, o_ref, kbuf_ref, vbuf_ref, tab_ref,
                      *, tq, seq):
    b = pl.program_id(0)
    i = pl.program_id(1)
    nblk = tq // BLOCK
    rows = SWA_GROUP * BLOCK
    win = 2 * BLOCK

    @pl.when(jnp.logical_and(b == 0, i == 0))
    def _build_tables():
        row = lax.broadcasted_iota(jnp.int32, (rows, win), 0)
        col = lax.broadcasted_iota(jnp.int32, (rows, win), 1)
        head = lax.shift_right_logical(row, 7)
        dist = jnp.bitwise_and(row, BLOCK - 1) + BLOCK - col
        valid = jnp.logical_and(dist >= 0, dist < BLOCK)
        has_key = jnp.logical_and(valid, col >= BLOCK)
        distf = dist.astype(F32)
        for g in range(SWA_KV_HEADS):
            def per_head(base):
                h0 = SWA_GROUP * g
                return jnp.where(head == 0, scal_ref[base + h0],
                                 jnp.where(head == 1, scal_ref[base + h0 + 1],
                                           jnp.where(head == 2, scal_ref[base + h0 + 2],
                                                     scal_ref[base + h0 + 3])))
            t = per_head(0) * distf + per_head(2 * SWA_Q_HEADS)
            tab_ref[g] = jnp.where(valid, t, MASKED_LOGIT_SHIFT)
            tab_ref[SWA_KV_HEADS + g] = jnp.where(has_key, t, MASKED_LOGIT_SHIFT)

    @pl.when(i == 0)
    def _stage_kv():
        kbuf_ref[:, 0:BLOCK, :] = jnp.zeros((SWA_KV_HEADS, BLOCK, LANES), BF16)
        vbuf_ref[:, 0:BLOCK, :] = jnp.zeros((SWA_KV_HEADS, BLOCK, 2 * LANES), BF16)
        lane = lax.broadcasted_iota(jnp.int32, (SWA_FILL, LANES), 1)
        first = lane < HEAD_DIM

        def body(c, carry):
            r0 = pl.multiple_of(c * SWA_FILL, SWA_FILL)
            dst = pl.ds(pl.multiple_of(r0 + BLOCK, BLOCK), SWA_FILL)
            for src_ref, dst_ref in ((k_ref, kbuf_ref), (v_ref, vbuf_ref)):
                x = src_ref[pl.ds(r0, SWA_FILL), :].astype(F32)
                xr = pltpu.roll(x, HEAD_DIM, 1)
                dst_ref[0, dst, 0:LANES] = jnp.where(first, x, xr).astype(BF16)
                dst_ref[1, dst, 0:LANES] = jnp.where(first, xr, x).astype(BF16)
            vbuf_ref[:, dst, LANES:2 * LANES] = jnp.ones((SWA_KV_HEADS, SWA_FILL, LANES), BF16)
            return carry
        lax.fori_loop(0, seq // SWA_FILL, body, 0)

    lane = lax.broadcasted_iota(jnp.int32, (BLOCK, LANES), 1)
    first = lane < HEAD_DIM
    for r in range(nblk):
        n = i * nblk + r
        w0 = pl.multiple_of(n * BLOCK, BLOCK)
        tsel = jnp.where(n == 0, SWA_KV_HEADS, 0)
        rs = slice(r * BLOCK, (r + 1) * BLOCK)
        for g in range(SWA_KV_HEADS):
            slabs = [q_ref[rs, (2 * g + a) * LANES:(2 * g + a + 1) * LANES] for a in range(2)]
            zero = jnp.zeros_like(slabs[0])
            lhs = jnp.concatenate(
                [jnp.where(first, slabs[0], zero), jnp.where(first, zero, slabs[0]),
                 jnp.where(first, slabs[1], zero), jnp.where(first, zero, slabs[1])], axis=0)
            s = _dot_nt(lhs, kbuf_ref[g, pl.ds(w0, win), :])
            p = jnp.exp2(s - tab_ref[tsel + g]).astype(BF16)
            out = _dot(p, vbuf_ref[g, pl.ds(w0, win), :])
            for a in range(2):
                halves = []
                for k in range(2):
                    hq = SWA_GROUP * g + 2 * a + k
                    blk = out[(2 * a + k) * BLOCK:(2 * a + k + 1) * BLOCK]
                    denom = blk[:, LANES:2 * LANES] + scal_ref[3 * SWA_Q_HEADS + hq]
                    halves.append(blk[:, 0:LANES] / denom)
                o_ref[rs, (2 * g + a) * LANES:(2 * g + a + 1) * LANES] = (
                    jnp.where(first, halves[0], halves[1]).astype(BF16))


def _swa_online_kernel(scal_ref, q_ref, k_ref, v_ref, o_ref, *, tq):
    i = pl.program_id(1)
    qi = lax.broadcasted_iota(jnp.int32, (BLOCK, BLOCK), 0)
    kj = lax.broadcasted_iota(jnp.int32, (BLOCK, BLOCK), 1)
    lower = kj <= qi
    dist = jnp.where(lower, qi - kj, BLOCK + qi - kj).astype(F32)
    for r in range(tq // BLOCK):
        n = i * (tq // BLOCK) + r
        cur = pl.multiple_of(n * BLOCK, BLOCK)
        prev = pl.multiple_of(jnp.maximum(n - 1, 0) * BLOCK, BLOCK)
        valid = jnp.logical_or(lower, n > 0)
        k_cur = k_ref[pl.ds(cur, BLOCK), :]
        k_prev = k_ref[pl.ds(prev, BLOCK), :]
        v_cur = v_ref[pl.ds(cur, BLOCK), :]
        v_prev = v_ref[pl.ds(prev, BLOCK), :]
        outs = []
        for hq in range(SWA_Q_HEADS):
            g = hq // SWA_GROUP
            gs = slice(g * HEAD_DIM, (g + 1) * HEAD_DIM)
            qh = q_ref[r * BLOCK:(r + 1) * BLOCK, hq * HEAD_DIM:(hq + 1) * HEAD_DIM]
            s_cur = _dot_nt(qh, k_cur[:, gs])
            s_prev = _dot_nt(qh, k_prev[:, gs])
            s = jnp.where(lower, s_cur, s_prev) - scal_ref[hq] * dist
            s = jnp.where(valid, s, NEG)
            sink = scal_ref[SWA_Q_HEADS + hq]
            m = jnp.maximum(jnp.max(s, axis=-1, keepdims=True), sink)
            e = jnp.exp2(s - m)
            denom = jnp.sum(e, axis=-1, keepdims=True) + jnp.exp2(sink - m)
            p = e / denom
            p_cur = jnp.where(lower, p, 0.0).astype(BF16)
            p_prev = jnp.where(lower, 0.0, p).astype(BF16)
            outs.append(_dot(p_cur, v_cur[:, gs]) + _dot(p_prev, v_prev[:, gs]))
        o_ref[r * BLOCK:(r + 1) * BLOCK, :] = jnp.concatenate(outs, axis=-1).astype(BF16)


def _swa(qa, ka, va, scal, *, bsz, seq, tq, fixed_reference):
    nq = seq // tq
    nbytes = 2 * (2 * tq * 512 * 2 + 2 * seq * LANES * 2)
    if fixed_reference:
        kernel = functools.partial(_swa_fixed_kernel, tq=tq, seq=seq)
        rows = SWA_GROUP * BLOCK
        scratch = [
            pltpu.VMEM((SWA_KV_HEADS, seq + BLOCK, LANES), BF16),
            pltpu.VMEM((SWA_KV_HEADS, seq + BLOCK, 2 * LANES), BF16),
            pltpu.VMEM((2 * SWA_KV_HEADS, rows, 2 * BLOCK), F32),
        ]
        nbytes += (SWA_KV_HEADS * (seq + BLOCK) * 3 * LANES * 2 + 2 * SWA_KV_HEADS * rows * 2 * BLOCK * 4
                   + 8 * rows * 2 * BLOCK * 4)
        name = "swa"
    else:
        kernel = functools.partial(_swa_online_kernel, tq=tq)
        scratch = []
        name = "swa_online"
    return pl.pallas_call(
        kernel,
        grid=(bsz, nq),
        in_specs=[
            pl.BlockSpec(memory_space=pltpu.SMEM),
            pl.BlockSpec((tq, 512), lambda b, i: (b * nq + i, 0)),
            pl.BlockSpec((seq, LANES), lambda b, i: (b, 0)),
            pl.BlockSpec((seq, LANES), lambda b, i: (b, 0)),
        ],
        out_specs=pl.BlockSpec((tq, 512), lambda b, i: (b * nq + i, 0)),
        out_shape=jax.ShapeDtypeStruct(qa.shape, BF16),
        scratch_shapes=scratch,
        compiler_params=pltpu.CompilerParams(
            dimension_semantics=("arbitrary", "arbitrary"), vmem_limit_bytes=_vmem_limit(nbytes)),
        name=name,
    )(scal, qa, ka, va)


def _stack_q(q, tq):
    lane = lax.broadcasted_iota(jnp.int32, (tq, LANES), 1)
    zero = jnp.zeros_like(q)
    return jnp.where(lane < HEAD_DIM, q, zero), jnp.where(lane < HEAD_DIM, zero, q)


def _diff_finish(o1, o2, lam_init, lp_ref, g_ref):
    lp = lp_ref[...]
    lam = (jnp.exp(jnp.sum(lp[0:1] * lp[1:2], axis=-1, keepdims=True))
           - jnp.exp(jnp.sum(lp[2:3] * lp[3:4], axis=-1, keepdims=True)) + lam_init)
    o = o1 - lam * o2
    ms = jnp.mean(o * o, axis=-1, keepdims=True)
    return (o * lax.rsqrt(ms + EPS) * g_ref[...] * (1.0 - lam_init)).astype(BF16)


def _causal_keep(i, tq, start, width):
    col = start + lax.broadcasted_iota(jnp.int32, (1, width), 1)
    row = i * tq + lax.rem(lax.broadcasted_iota(jnp.int32, (2 * tq, 1), 0), tq)
    return col <= row


def _aug_lanes(val, ones_first):
    hi = val.astype(BF16).astype(F32)
    r1 = val - hi
    mid = r1.astype(BF16).astype(F32)
    lo = (r1 - mid).astype(BF16).astype(F32)
    lane = lax.broadcasted_iota(jnp.int32, val.shape, 1)
    base = 3 if ones_first else 0
    ones_lo = 0 if ones_first else 3
    parts = jnp.where(lane == base, hi, jnp.where(lane == base + 1, mid,
                                                  jnp.where(lane == base + 2, lo, 0.0)))
    is_one = jnp.logical_and(lane >= ones_lo, lane < ones_lo + 3)
    return jnp.where(is_one, 1.0, parts).astype(BF16)


def _diff_fixed_kernel(scal_ref, q_ref, k_ref, v_ref, lp_ref, g_ref, o_ref,
                       kaug_ref, vaug_ref, qs_ref, acc_ref, *, tq, seq):
    h = pl.program_id(1)
    i = pl.program_id(2)
    lam_init = scal_ref[0]
    bound2 = scal_ref[1]
    slope2 = scal_ref[2 + h]
    half = tq // 2

    @pl.when(i == 0)
    def _build_kv():
        def body(c, carry):
            r0 = pl.multiple_of(c * half, half)
            pos = (r0 + lax.broadcasted_iota(jnp.int32, (half, LANES), 0)).astype(F32)
            kaug_ref[pl.ds(r0, half), 0:LANES] = k_ref[pl.ds(r0, half), :]
            kaug_ref[pl.ds(r0, half), LANES:2 * LANES] = _aug_lanes(slope2 * pos, ones_first=False)
            vaug_ref[pl.ds(r0, half), 0:LANES] = v_ref[pl.ds(r0, half), :]
            vaug_ref[pl.ds(r0, half), LANES:2 * LANES] = jnp.ones((half, LANES), BF16)
            return carry
        lax.fori_loop(0, seq // half, body, 0)

    for part in range(2):
        q1, q2 = _stack_q(q_ref[part * half:(part + 1) * half, :], half)
        rowpos = (i * tq + part * half
                  + lax.broadcasted_iota(jnp.int32, (half, LANES), 0)).astype(F32)
        aug = _aug_lanes(-(slope2 * rowpos + bound2), ones_first=True)
        base = part * tq
        qs_ref[base:base + half, 0:LANES] = q1
        qs_ref[base + half:base + tq, 0:LANES] = q2
        qs_ref[base:base + half, LANES:2 * LANES] = aug
        qs_ref[base + half:base + tq, LANES:2 * LANES] = aug
    acc_ref[...] = jnp.zeros(acc_ref.shape, F32)

    def full_block(j, carry):
        start = pl.multiple_of(j * tq, tq)
        p = jnp.exp2(_dot_nt(qs_ref[...], kaug_ref[pl.ds(start, tq), :]))
        acc_ref[...] += _dot(p.astype(BF16), vaug_ref[pl.ds(start, tq), :])
        return carry

    lax.fori_loop(0, i, full_block, 0)

    d0 = pl.multiple_of(i * tq, tq)
    d1 = pl.multiple_of(i * tq + half, half)
    col = lax.broadcasted_iota(jnp.int32, (1, half), 1)
    srow = lax.broadcasted_iota(jnp.int32, (2 * tq, 1), 0)
    row = jnp.bitwise_and(srow, half - 1) + jnp.where(srow >= tq, half, 0)
    p0 = jnp.exp2(_dot_nt(qs_ref[...], kaug_ref[pl.ds(d0, half), :]))
    p0 = jnp.where(col <= row, p0, 0.0).astype(BF16)
    p1 = jnp.exp2(_dot_nt(qs_ref[tq:2 * tq, :], kaug_ref[pl.ds(d1, half), :]))
    p1 = jnp.where(col + half <= row[tq:2 * tq], p1, 0.0).astype(BF16)
    acc_ref[...] += _dot(p0, vaug_ref[pl.ds(d0, half), :])
    acc_ref[tq:2 * tq, :] += _dot(p1, vaug_ref[pl.ds(d1, half), :])

    for part in range(2):
        base = part * tq
        a1 = acc_ref[base:base + half, :]
        a2 = acc_ref[base + half:base + tq, :]
        o_ref[part * half:(part + 1) * half, :] = _diff_finish(
            a1[:, 0:LANES] / a1[:, LANES:2 * LANES], a2[:, 0:LANES] / a2[:, LANES:2 * LANES],
            lam_init, lp_ref, g_ref)


def _diff_online_kernel(scal_ref, q_ref, k_ref, v_ref, lp_ref, g_ref, o_ref,
                        qs_ref, m_ref, l_ref, acc_ref, *, tq):
    h = pl.program_id(1)
    i = pl.program_id(2)
    lam_init = scal_ref[0]
    slope2 = scal_ref[2 + h]

    q1, q2 = _stack_q(q_ref[...], tq)
    qs_ref[0:tq, :] = q1
    qs_ref[tq:2 * tq, :] = q2
    m_ref[...] = jnp.full(m_ref.shape, NEG, F32)
    l_ref[...] = jnp.zeros(l_ref.shape, F32)
    acc_ref[...] = jnp.zeros(acc_ref.shape, F32)

    def step(j, masked):
        start = pl.multiple_of(j * tq, tq)
        s = _dot_nt(qs_ref[...], k_ref[pl.ds(start, tq), :])
        col = start + lax.broadcasted_iota(jnp.int32, (1, tq), 1)
        s = s + slope2 * col.astype(F32)
        if masked:
            s = jnp.where(_causal_keep(i, tq, start, tq), s, NEG)
        m_prev = m_ref[...]
        m_new = jnp.maximum(m_prev, jnp.max(s, axis=-1, keepdims=True))
        alpha = jnp.exp2(m_prev - m_new)
        p = jnp.exp2(s - m_new)
        l_ref[...] = alpha * l_ref[...] + jnp.sum(p, axis=-1, keepdims=True)
        acc_ref[...] = alpha * acc_ref[...] + _dot(p.astype(BF16), v_ref[pl.ds(start, tq), :])
        m_ref[...] = m_new

    def body(j, carry):
        step(j, False)
        return carry

    lax.fori_loop(0, i, body, 0)
    step(i, True)

    acc = acc_ref[...]
    l = l_ref[...]
    o_ref[...] = _diff_finish(acc[0:tq] / l[0:tq], acc[tq:2 * tq] / l[tq:2 * tq],
                              lam_init, lp_ref, g_ref)


def _diff(qd, kd, vd, scal, lp, gsub, *, bsz, seq, tq, fixed_reference):
    nq = seq // tq
    t = bsz * seq
    if fixed_reference:
        kernel = functools.partial(_diff_fixed_kernel, tq=tq, seq=seq)
        scratch = [
            pltpu.VMEM((seq, 2 * LANES), BF16),
            pltpu.VMEM((seq, 2 * LANES), BF16),
            pltpu.VMEM((2 * tq, 2 * LANES), BF16),
            pltpu.VMEM((2 * tq, 2 * LANES), F32),
        ]
        scratch_bytes = 2 * seq * 2 * LANES * 2 + 2 * tq * 2 * LANES * 6 + 2 * tq * tq * 6
        name = "diff"
    else:
        kernel = functools.partial(_diff_online_kernel, tq=tq)
        scratch = [
            pltpu.VMEM((2 * tq, LANES), BF16),
            pltpu.VMEM((2 * tq, 1), F32),
            pltpu.VMEM((2 * tq, 1), F32),
            pltpu.VMEM((2 * tq, LANES), F32),
        ]
        scratch_bytes = 2 * tq * (LANES * 6 + 8) + 6 * 2 * tq * tq * 4
        name = "diff_online"
    nbytes = 2 * (2 * seq * LANES * 2 + 2 * tq * LANES * 2) + scratch_bytes
    return pl.pallas_call(
        kernel,
        grid=(bsz, DIFF_HEADS, nq),
        in_specs=[
            pl.BlockSpec(memory_space=pltpu.SMEM),
            pl.BlockSpec((None, tq, LANES), lambda b, h, i: (h, b * nq + i, 0)),
            pl.BlockSpec((None, seq, LANES), lambda b, h, i: (h, b, 0)),
            pl.BlockSpec((None, seq, LANES), lambda b, h, i: (h, b, 0)),
            pl.BlockSpec((4, HEAD_DIM), lambda b, h, i: (0, 0)),
            pl.BlockSpec((1, DIFF_V_DIM), lambda b, h, i: (0, 0)),
        ],
        out_specs=pl.BlockSpec((tq, LANES), lambda b, h, i: (b * nq + i, h)),
        out_shape=jax.ShapeDtypeStruct((t, DIFF_HEADS * DIFF_V_DIM), BF16),
        scratch_shapes=scratch,
        compiler_params=pltpu.CompilerParams(
            dimension_semantics=("arbitrary", "arbitrary", "arbitrary"),
            vmem_limit_bytes=_vmem_limit(nbytes)),
        name=name,
    )(scal, qd, kd, vd, lp, gsub)


def _merge_kernel(x_ref, oa_ref, ob_ref, gate_ref, wb_ref, wo_ref, out_ref):
    d = x_ref.shape[1]
    up_a = _dot(oa_ref[...], wb_ref[0])
    up_b = _dot(ob_ref[...], wb_ref[1])
    merged = gate_ref[:, :d].astype(F32) * up_a + gate_ref[:, d:].astype(F32) * up_b
    out_ref[...] = x_ref[...] + _dot(merged.astype(BF16), wo_ref[...])


def _merge(x, oa, ob, gates, wb, wo, *, tm):
    t, d = x.shape
    row = lambda i: (i, 0)
    nbytes = 2 * (2 * tm * d * 4 + 2 * tm * 512 * 2 + tm * 2 * d * 2 + 2 * 512 * d * 2 + d * d * 2)
    return pl.pallas_call(
        _merge_kernel,
        grid=(t // tm,),
        in_specs=[
            pl.BlockSpec((tm, d), row),
            pl.BlockSpec((tm, 512), row),
            pl.BlockSpec((tm, 512), row),
            pl.BlockSpec((tm, 2 * d), row),
            pl.BlockSpec(wb.shape, lambda i: (0, 0, 0)),
            pl.BlockSpec(wo.shape, lambda i: (0, 0)),
        ],
        out_specs=pl.BlockSpec((tm, d), row),
        out_shape=jax.ShapeDtypeStruct(x.shape, F32),
        compiler_params=pltpu.CompilerParams(
            dimension_semantics=("arbitrary",), vmem_limit_bytes=_vmem_limit(nbytes)),
        name="merge",
    )(x, oa, ob, gates, wb, wo)


def _ffn_kernel(x_ref, gn_ref, win_ref, wout_ref, out_ref, h_ref, *, chunk):
    x = x_ref[...]
    ms = jnp.mean(x * x, axis=-1, keepdims=True)
    h_ref[...] = (x * lax.rsqrt(ms + EPS) * gn_ref[...]).astype(BF16)
    hidden = wout_ref.shape[0]
    acc = x
    for c in range(hidden // chunk):
        g = _dot(h_ref[...], win_ref[:, c * chunk:(c + 1) * chunk])
        u = _dot(h_ref[...], win_ref[:, hidden + c * chunk:hidden + (c + 1) * chunk])
        a = (g * jax.nn.sigmoid(g) * u).astype(BF16)
        acc = acc + _dot(a, wout_ref[c * chunk:(c + 1) * chunk, :])
    out_ref[...] = acc


def _ffn(x, gn, win, wout, *, tm, chunk):
    t, d = x.shape
    hidden = wout.shape[0]
    row = lambda i: (i, 0)
    const = lambda i: (0, 0)
    kernel = functools.partial(_ffn_kernel, chunk=chunk)
    nbytes = 2 * (2 * tm * d * 4 + 3 * d * hidden * 2) + tm * d * 2 + 4 * tm * chunk * 4
    return pl.pallas_call(
        kernel,
        grid=(t // tm,),
        in_specs=[
            pl.BlockSpec((tm, d), row),
            pl.BlockSpec((1, d), const),
            pl.BlockSpec((d, 2 * hidden), const),
            pl.BlockSpec((hidden, d), const),
        ],
        out_specs=pl.BlockSpec((tm, d), row),
        out_shape=jax.ShapeDtypeStruct(x.shape, F32),
        scratch_shapes=[pltpu.VMEM((tm, d), BF16)],
        compiler_params=pltpu.CompilerParams(
            dimension_semantics=("arbitrary",), vmem_limit_bytes=_vmem_limit(nbytes)),
        name="ffn",
    )(x, gn, win, wout)


def _alibi_slopes():
    return jnp.exp2(-8.0 * jnp.arange(1, N_ALIBI_HEADS + 1, dtype=F32) / N_ALIBI_HEADS)


def _group_mean_matrix():
    idx = jnp.arange(256) // HEAD_DIM
    return jnp.where(idx[:, None] == idx[None, :], 1.0 / HEAD_DIM, 0.0).astype(BF16)


def kernel(x, w_in, b_gate, w_branch, w_o, norm_mix, norm_ffn, qk_norm_swa, qk_norm_diff,
           attn_sinks, diff_lambda, diff_subln, w_ffn_in, w_ffn_out):
    bsz, seq, d = x.shape
    depth = w_in.shape[0]
    t = bsz * seq
    slopes = _alibi_slopes()
    slopes_swa, slopes_diff = slopes[:SWA_Q_HEADS], slopes[SWA_Q_HEADS:]
    bd = _group_mean_matrix()
    ones128 = jnp.ones((128,), F32)

    xf = x.reshape(t, d)
    for l in range(depth):
        gq_a = jnp.tile(qk_norm_swa[l, 0], SWA_Q_HEADS) * (QK_SCALE * LOG2E)
        gk_a = jnp.tile(qk_norm_swa[l, 1], SWA_KV_HEADS)
        gq_d = jnp.tile(qk_norm_diff[l, 0], 2 * DIFF_HEADS) * (QK_SCALE * LOG2E)
        gk_d = jnp.tile(qk_norm_diff[l, 1], 2 * DIFF_HEADS)
        gcol = jnp.concatenate([gq_a, gq_d, gk_d, gk_a, ones128]).reshape(1, -1).astype(F32)
        wl = w_in[l]
        w_perm = jnp.concatenate(
            [wl[:, 0:512], wl[:, 768:1792], wl[:, 512:768], wl[:, 1792:]], axis=1).astype(BF16)

        qa, ka, va, qd, kd, vd, gates = _proj(
            xf, norm_mix[l].reshape(1, d), w_perm, gcol, bd,
            b_gate[l].reshape(1, 2 * d), tm=512)

        bound_a = (SCORE_BOUND_PER_GAIN * jnp.max(jnp.abs(qk_norm_swa[l, 0]))
                   * jnp.max(jnp.abs(qk_norm_swa[l, 1]))).astype(F32)
        sinks2 = attn_sinks[l].astype(F32) * LOG2E
        ref_a = jnp.maximum(bound_a * LOG2E, sinks2)
        scal_a = jnp.concatenate([slopes_swa * LOG2E, sinks2, ref_a, jnp.exp2(sinks2 - ref_a)]).astype(F32)
        o_a = lax.cond(
            bound_a <= FIXED_REF_MAX_BOUND,
            lambda a: _swa(*a, bsz=bsz, seq=seq, tq=512, fixed_reference=True),
            lambda a: _swa(*a, bsz=bsz, seq=seq, tq=512, fixed_reference=False),
            (qa, ka, va, scal_a))

        lam_init = 0.8 - 0.6 * math.exp(-0.3 * l)
        bound = (SCORE_BOUND_PER_GAIN * jnp.max(jnp.abs(qk_norm_diff[l, 0]))
                 * jnp.max(jnp.abs(qk_norm_diff[l, 1]))).astype(F32)
        scal = jnp.concatenate([jnp.array([lam_init], F32), (bound * LOG2E).reshape(1),
                                slopes_diff * LOG2E]).astype(F32)
        diff_args = (qd, kd, vd, scal, diff_lambda[l].astype(F32), diff_subln[l].reshape(1, DIFF_V_DIM))
        o_b = lax.cond(
            bound <= FIXED_REF_MAX_BOUND,
            lambda a: _diff(*a, bsz=bsz, seq=seq, tq=1024, fixed_reference=True),
            lambda a: _diff(*a, bsz=bsz, seq=seq, tq=256, fixed_reference=False),
            diff_args)

        xf = _merge(xf, o_a, o_b, gates, w_branch[l].astype(BF16), w_o[l].astype(BF16), tm=512)

        xf = _ffn(xf, norm_ffn[l].reshape(1, d), w_ffn_in[l].astype(BF16),
                  w_ffn_out[l].astype(BF16), tm=512, chunk=256)
    return xf.reshape(bsz, seq, d)
```

```python
import functools
import math

import jax
import jax.numpy as jnp
from jax import lax
from jax.experimental import pallas as pl
from jax.experimental.pallas import tpu as pltpu

F32 = jnp.float32
BF16 = jnp.bfloat16

HEAD_DIM = 64
BLOCK = 128
SWA_Q_HEADS = 8
SWA_KV_HEADS = 2
SWA_GROUP = SWA_Q_HEADS // SWA_KV_HEADS
DIFF_HEADS = 4
DIFF_V_DIM = 2 * HEAD_DIM
N_ALIBI_HEADS = SWA_Q_HEADS + DIFF_HEADS
NEG = -1e30
EPS = 1e-6
QK_SCALE = HEAD_DIM ** -0.5
LOG2E = math.log2(math.e)
SCORE_BOUND_PER_GAIN = HEAD_DIM * QK_SCALE
FIXED_REF_MAX_BOUND = 20.0

V7X_VMEM_BYTES = 64 * 1024 * 1024
LANES = 128


def _vmem_limit(nbytes):
    return int(min(max(2 * nbytes, 32 * 1024 * 1024), V7X_VMEM_BYTES - 8 * 1024 * 1024))


def _dot(a, b):
    return jnp.dot(a, b, preferred_element_type=F32)


def _dot_nt(a, b):
    return lax.dot_general(a, b, (((1,), (1,)), ((), ())), preferred_element_type=F32)


PROJ_CHUNK = 256
PROJ_NORMED = 1792


def _proj_kernel(x_ref, gmix_ref, w_ref, gcol_ref, bd_ref, bg_ref,
                 qa_ref, ka_ref, va_ref, qd_ref, kd_ref, vd_ref, gate_ref, h_ref, y_ref):
    tm = x_ref.shape[0]
    x = x_ref[...]
    ms = jnp.mean(x * x, axis=-1, keepdims=True)
    h_ref[...] = (x * lax.rsqrt(ms + EPS) * gmix_ref[...]).astype(BF16)

    def proj(c0, width):
        return _dot(h_ref[...], w_ref[:, c0:c0 + width])

    nchunk = PROJ_NORMED // PROJ_CHUNK
    y_ref[...] = proj(0, PROJ_NORMED)
    sq = jnp.concatenate(
        [jnp.square(y_ref[:, PROJ_CHUNK * c:PROJ_CHUNK * (c + 1)]).astype(BF16) for c in range(nchunk)],
        axis=0)
    inv = lax.rsqrt(_dot(sq, bd_ref[...]) + EPS)

    def normed(c):
        cs = slice(PROJ_CHUNK * c, PROJ_CHUNK * (c + 1))
        return (y_ref[:, cs] * inv[c * tm:(c + 1) * tm] * gcol_ref[:, cs]).astype(BF16)

    for c in range(2):
        qa_ref[:, PROJ_CHUNK * c:PROJ_CHUNK * (c + 1)] = normed(c)
    for c in range(2):
        y = normed(2 + c)
        qd_ref[2 * c] = y[:, :LANES]
        qd_ref[2 * c + 1] = y[:, LANES:]
    for c in range(2):
        y = normed(4 + c)
        kd_ref[2 * c] = y[:, :LANES]
        kd_ref[2 * c + 1] = y[:, LANES:]
    ka_ref[...] = normed(6)[:, :LANES]
    va_ref[...] = y_ref[:, PROJ_NORMED - LANES:PROJ_NORMED].astype(BF16)
    y = proj(PROJ_NORMED, DIFF_HEADS * LANES).astype(BF16)
    for hd in range(DIFF_HEADS):
        vd_ref[hd] = y[:, hd * LANES:(hd + 1) * LANES]
    g0 = PROJ_NORMED + DIFF_HEADS * LANES
    gate_ref[...] = jax.nn.sigmoid(proj(g0, gate_ref.shape[1]) + bg_ref[...]).astype(BF16)


def _proj(x, gmix, w, gcol, bd, bg, *, tm):
    t, d = x.shape
    ncols = w.shape[1]
    grid = (t // tm,)
    const = lambda i: (0, 0)
    out_shape = (
        jax.ShapeDtypeStruct((t, 512), BF16),
        jax.ShapeDtypeStruct((t, 128), BF16),
        jax.ShapeDtypeStruct((t, 128), BF16),
        jax.ShapeDtypeStruct((DIFF_HEADS, t, 128), BF16),
        jax.ShapeDtypeStruct((DIFF_HEADS, t, 128), BF16),
        jax.ShapeDtypeStruct((DIFF_HEADS, t, 128), BF16),
        jax.ShapeDtypeStruct((t, 2 * d), BF16),
    )
    row = lambda i: (i, 0)
    hrow = lambda i: (0, i, 0)
    out_specs = (
        pl.BlockSpec((tm, 512), row),
        pl.BlockSpec((tm, 128), row),
        pl.BlockSpec((tm, 128), row),
        pl.BlockSpec((DIFF_HEADS, tm, 128), hrow),
        pl.BlockSpec((DIFF_HEADS, tm, 128), hrow),
        pl.BlockSpec((DIFF_HEADS, tm, 128), hrow),
        pl.BlockSpec((tm, 2 * d), row),
    )
    in_specs = [
        pl.BlockSpec((tm, d), row),
        pl.BlockSpec((1, d), const),
        pl.BlockSpec((d, ncols), const),
        pl.BlockSpec((1, gcol.shape[1]), const),
        pl.BlockSpec(bd.shape, const),
        pl.BlockSpec((1, 2 * d), const),
    ]
    nbytes = 2 * (tm * d * 4 + d * ncols * 2 + tm * ncols * 2) + tm * d * 2 + 3 * tm * PROJ_NORMED * 4
    return pl.pallas_call(
        _proj_kernel,
        grid=grid,
        in_specs=in_specs,
        out_specs=out_specs,
        out_shape=out_shape,
        scratch_shapes=[pltpu.VMEM((tm, d), BF16), pltpu.VMEM((tm, PROJ_NORMED), F32)],
        compiler_params=pltpu.CompilerParams(
            dimension_semantics=("arbitrary",), vmem_limit_bytes=_vmem_limit(nbytes)),
        name="proj",
    )(x, gmix, w, gcol, bd, bg)


SWA_FILL = 512
MASKED_LOGIT_SHIFT = 1e30


def _swa_fixed_kernel(scal_ref, q_ref, k_ref, v_ref, o_ref, kbuf_ref, vbuf_ref, tab_ref,
                      *, tq, seq):
    b = pl.program_id(0)
    i = pl.program_id(1)
    nblk = tq // BLOCK
    rows = SWA_GROUP * BLOCK
    win = 2 * BLOCK

    @pl.when(jnp.logical_and(b == 0, i == 0))
    def _build_tables():
        row = lax.broadcasted_iota(jnp.int32, (rows, win), 0)
        col = lax.broadcasted_iota(jnp.int32, (rows, win), 1)
        head = lax.shift_right_logical(row, 7)
        dist = jnp.bitwise_and(row, BLOCK - 1) + BLOCK - col
        valid = jnp.logical_and(dist >= 0, dist < BLOCK)
        has_key = jnp.logical_and(valid, col >= BLOCK)
        distf = dist.astype(F32)
        for g in range(SWA_KV_HEADS):
            def per_head(base):
                h0 = SWA_GROUP * g
                return jnp.where(head == 0, scal_ref[base + h0],
                                 jnp.where(head == 1, scal_ref[base + h0 + 1],
                                           jnp.where(head == 2, scal_ref[base + h0 + 2],
                                                     scal_ref[base + h0 + 3])))
            t = per_head(0) * distf + per_head(2 * SWA_Q_HEADS)
            tab_ref[g] = jnp.where(valid, t, MASKED_LOGIT_SHIFT)
            tab_ref[SWA_KV_HEADS + g] = jnp.where(has_key, t, MASKED_LOGIT_SHIFT)

    @pl.when(i == 0)
    def _stage_kv():
        kbuf_ref[:, 0:BLOCK, :] = jnp.zeros((SWA_KV_HEADS, BLOCK, LANES), BF16)
        vbuf_ref[:, 0:BLOCK, :] = jnp.zeros((SWA_KV_HEADS, BLOCK, 2 * LANES), BF16)
        lane = lax.broadcasted_iota(jnp.int32, (SWA_FILL, LANES), 1)
        first = lane < HEAD_DIM

        def body(c, carry):
            r0 = pl.multiple_of(c * SWA_FILL, SWA_FILL)
            dst = pl.ds(pl.multiple_of(r0 + BLOCK, BLOCK), SWA_FILL)
            for src_ref, dst_ref in ((k_ref, kbuf_ref), (v_ref, vbuf_ref)):
                x = src_ref[pl.ds(r0, SWA_FILL), :].astype(F32)
                xr = pltpu.roll(x, HEAD_DIM, 1)
                dst_ref[0, dst, 0:LANES] = jnp.where(first, x, xr).astype(BF16)
                dst_ref[1, dst, 0:LANES] = jnp.where(first, xr, x).astype(BF16)
            vbuf_ref[:, dst, LANES:2 * LANES] = jnp.ones((SWA_KV_HEADS, SWA_FILL, LANES), BF16)
            return carry
        lax.fori_loop(0, seq // SWA_FILL, body, 0)

    lane = lax.broadcasted_iota(jnp.int32, (BLOCK, LANES), 1)
    first = lane < HEAD_DIM
    for r in range(nblk):
        n = i * nblk + r
        w0 = pl.multiple_of(n * BLOCK, BLOCK)
        tsel = jnp.where(n == 0, SWA_KV_HEADS, 0)
        rs = slice(r * BLOCK, (r + 1) * BLOCK)
        for g in range(SWA_KV_HEADS):
            slabs = [q_ref[rs, (2 * g + a) * LANES:(2 * g + a + 1) * LANES] for a in range(2)]
            zero = jnp.zeros_like(slabs[0])
            lhs = jnp.concatenate(
                [jnp.where(first, slabs[0], zero), jnp.where(first, zero, slabs[0]),
                 jnp.where(first, slabs[1], zero), jnp.where(first, zero, slabs[1])], axis=0)
            s = _dot_nt(lhs, kbuf_ref[g, pl.ds(w0, win), :])
            p = jnp.exp2(s - tab_ref[tsel + g]).astype(BF16)
            out = _dot(p, vbuf_ref[g, pl.ds(w0, win), :])
            for a in range(2):
                halves = []
                for k in range(2):
                    hq = SWA_GROUP * g + 2 * a + k
                    blk = out[(2 * a + k) * BLOCK:(2 * a + k + 1) * BLOCK]
                    denom = blk[:, LANES:2 * LANES] + scal_ref[3 * SWA_Q_HEADS + hq]
                    halves.append(blk[:, 0:LANES] / denom)
                o_ref[rs, (2 * g + a) * LANES:(2 * g + a + 1) * LANES] = (
                    jnp.where(first, halves[0], halves[1]).astype(BF16))


def _swa_online_kernel(scal_ref, q_ref, k_ref, v_ref, o_ref, *, tq):
    i = pl.program_id(1)
    qi = lax.broadcasted_iota(jnp.int32, (BLOCK, BLOCK), 0)
    kj = lax.broadcasted_iota(jnp.int32, (BLOCK, BLOCK), 1)
    lower = kj <= qi
    dist = jnp.where(lower, qi - kj, BLOCK + qi - kj).astype(F32)
    for r in range(tq // BLOCK):
        n = i * (tq // BLOCK) + r
        cur = pl.multiple_of(n * BLOCK, BLOCK)
        prev = pl.multiple_of(jnp.maximum(n - 1, 0) * BLOCK, BLOCK)
        valid = jnp.logical_or(lower, n > 0)
        k_cur = k_ref[pl.ds(cur, BLOCK), :]
        k_prev = k_ref[pl.ds(prev, BLOCK), :]
        v_cur = v_ref[pl.ds(cur, BLOCK), :]
        v_prev = v_ref[pl.ds(prev, BLOCK), :]
        outs = []
        for hq in range(SWA_Q_HEADS):
            g = hq // SWA_GROUP
            gs = slice(g * HEAD_DIM, (g + 1) * HEAD_DIM)
            qh = q_ref[r * BLOCK:(r + 1) * BLOCK, hq * HEAD_DIM:(hq + 1) * HEAD_DIM]
            s_cur = _dot_nt(qh, k_cur[:, gs])
            s_prev = _dot_nt(qh, k_prev[:, gs])
            s = jnp.where(lower, s_cur, s_prev) - scal_ref[hq] * dist
            s = jnp.where(valid, s, NEG)
            sink = scal_ref[SWA_Q_HEADS + hq]
            m = jnp.maximum(jnp.max(s, axis=-1, keepdims=True), sink)
            e = jnp.exp2(s - m)
            denom = jnp.sum(e, axis=-1, keepdims=True) + jnp.exp2(sink - m)
            p = e / denom
            p_cur = jnp.where(lower, p, 0.0).astype(BF16)
            p_prev = jnp.where(lower, 0.0, p).astype(BF16)
            outs.append(_dot(p_cur, v_cur[:, gs]) + _dot(p_prev, v_prev[:, gs]))
        o_ref[r * BLOCK:(r + 1) * BLOCK, :] = jnp.concatenate(outs, axis=-1).astype(BF16)


def _swa(qa, ka, va, scal, *, bsz, seq, tq, fixed_reference):
    nq = seq // tq
    nbytes = 2 * (2 * tq * 512 * 2 + 2 * seq * LANES * 2)
    if fixed_reference:
        kernel = functools.partial(_swa_fixed_kernel, tq=tq, seq=seq)
        rows = SWA_GROUP * BLOCK
        scratch = [
            pltpu.VMEM((SWA_KV_HEADS, seq + BLOCK, LANES), BF16),
            pltpu.VMEM((SWA_KV_HEADS, seq + BLOCK, 2 * LANES), BF16),
            pltpu.VMEM((2 * SWA_KV_HEADS, rows, 2 * BLOCK), F32),
        ]
        nbytes += (SWA_KV_HEADS * (seq + BLOCK) * 3 * LANES * 2 + 2 * SWA_KV_HEADS * rows * 2 * BLOCK * 4
                   + 8 * rows * 2 * BLOCK * 4)
        name = "swa"
    else:
        kernel = functools.partial(_swa_online_kernel, tq=tq)
        scratch = []
        name = "swa_online"
    return pl.pallas_call(
        kernel,
        grid=(bsz, nq),
        in_specs=[
            pl.BlockSpec(memory_space=pltpu.SMEM),
            pl.BlockSpec((tq, 512), lambda b, i: (b * nq + i, 0)),
            pl.BlockSpec((seq, LANES), lambda b, i: (b, 0)),
            pl.BlockSpec((seq, LANES), lambda b, i: (b, 0)),
        ],
        out_specs=pl.BlockSpec((tq, 512), lambda b, i: (b * nq + i, 0)),
        out_shape=jax.ShapeDtypeStruct(qa.shape, BF16),
        scratch_shapes=scratch,
        compiler_params=pltpu.CompilerParams(
            dimension_semantics=("arbitrary", "arbitrary"), vmem_limit_bytes=_vmem_limit(nbytes)),
        name=name,
    )(scal, qa, ka, va)


def _stack_q(q, tq):
    lane = lax.broadcasted_iota(jnp.int32, (tq, LANES), 1)
    zero = jnp.zeros_like(q)
    return jnp.where(lane < HEAD_DIM, q, zero), jnp.where(lane < HEAD_DIM, zero, q)


def _diff_finish(o1, o2, lam_init, lp_ref, g_ref):
    lp = lp_ref[...]
    lam = (jnp.exp(jnp.sum(lp[0:1] * lp[1:2], axis=-1, keepdims=True))
           - jnp.exp(jnp.sum(lp[2:3] * lp[3:4], axis=-1, keepdims=True)) + lam_init)
    o = o1 - lam * o2
    ms = jnp.mean(o * o, axis=-1, keepdims=True)
    return (o * lax.rsqrt(ms + EPS) * g_ref[...] * (1.0 - lam_init)).astype(BF16)


def _causal_keep(i, tq, start, width):
    col = start + lax.broadcasted_iota(jnp.int32, (1, width), 1)
    row = i * tq + lax.rem(lax.broadcasted_iota(jnp.int32, (2 * tq, 1), 0), tq)
    return col <= row


AUG_K_BASE = 3
AUG_Q_BASE = 9


def _split3(val):
    hi = val.astype(BF16).astype(F32)
    r1 = val - hi
    mid = r1.astype(BF16).astype(F32)
    lo = (r1 - mid).astype(BF16).astype(F32)
    return hi, mid, lo


def _aug_templates(slope2, rows, tmpl_ref):
    r = lax.broadcasted_iota(jnp.int32, (rows, LANES), 0).astype(F32)
    hi, mid, lo = _split3(slope2 * r)
    lane = lax.broadcasted_iota(jnp.int32, (rows, LANES), 1)
    parts = jnp.where(jnp.logical_or(lane == 0, lane == 6), hi,
                      jnp.where(jnp.logical_or(lane == 1, lane == 7), mid, lo))
    k_ones = jnp.logical_and(lane >= 6, lane < 12)
    tmpl_ref[0] = jnp.where(lane < 3, parts, jnp.where(k_ones, 1.0, 0.0))
    tmpl_ref[1] = jnp.where(lane < 6, 1.0, jnp.where(lane < 9, -parts, 0.0))


def _with_scalar_lanes(template, value, base):
    hi, mid, lo = _split3(jnp.full((1, LANES), value, F32))
    lane = lax.broadcasted_iota(jnp.int32, (1, LANES), 1)
    terms = jnp.where(lane == base, hi, jnp.where(lane == base + 1, mid, lo))
    here = jnp.logical_and(lane >= base, lane < base + 3)
    return jnp.where(here, terms, template).astype(BF16)


def _diff_fixed_kernel(scal_ref, q_ref, k_ref, v_ref, lp_ref, g_ref, o_ref,
                       kaug_ref, vaug_ref, qs_ref, acc_ref, tmpl_ref, *, tq, seq):
    h = pl.program_id(1)
    i = pl.program_id(2)
    lam_init = scal_ref[0]
    bound2 = scal_ref[1]
    slope2 = scal_ref[2 + h]
    half = tq // 2

    @pl.when(i == 0)
    def _build_kv():
        _aug_templates(slope2, half, tmpl_ref)

        def body(c, carry):
            r0 = pl.multiple_of(c * half, half)
            kaug_ref[pl.ds(r0, half), 0:LANES] = k_ref[pl.ds(r0, half), :]
            kaug_ref[pl.ds(r0, half), LANES:2 * LANES] = _with_scalar_lanes(
                tmpl_ref[0], slope2 * r0.astype(F32), AUG_K_BASE)
            vaug_ref[pl.ds(r0, half), 0:LANES] = v_ref[pl.ds(r0, half), :]
            vaug_ref[pl.ds(r0, half), LANES:2 * LANES] = jnp.ones((half, LANES), BF16)
            return carry
        lax.fori_loop(0, seq // half, body, 0)

    for part in range(2):
        q1, q2 = _stack_q(q_ref[part * half:(part + 1) * half, :], half)
        row0 = (i * tq + part * half).astype(F32)
        aug = _with_scalar_lanes(tmpl_ref[1], -(slope2 * row0 + bound2), AUG_Q_BASE)
        base = part * tq
        qs_ref[base:base + half, 0:LANES] = q1
        qs_ref[base + half:base + tq, 0:LANES] = q2
        qs_ref[base:base + half, LANES:2 * LANES] = aug
        qs_ref[base + half:base + tq, LANES:2 * LANES] = aug
    acc_ref[...] = jnp.zeros(acc_ref.shape, F32)

    def full_block(j):
        start = pl.multiple_of(j * tq, tq)
        p = jnp.exp2(_dot_nt(qs_ref[...], kaug_ref[pl.ds(start, tq), :]))
        acc_ref[...] += _dot(p.astype(BF16), vaug_ref[pl.ds(start, tq), :])

    def block_pair(j, carry):
        full_block(2 * j)
        full_block(2 * j + 1)
        return carry

    lax.fori_loop(0, lax.shift_right_logical(i, 1), block_pair, 0)

    @pl.when(jnp.bitwise_and(i, 1) == 1)
    def _odd_block():
        full_block(i - 1)

    d0 = pl.multiple_of(i * tq, tq)
    d1 = pl.multiple_of(i * tq + half, half)
    col = lax.broadcasted_iota(jnp.int32, (1, half), 1)
    srow = lax.broadcasted_iota(jnp.int32, (2 * tq, 1), 0)
    row = jnp.bitwise_and(srow, half - 1) + jnp.where(srow >= tq, half, 0)
    p0 = jnp.exp2(_dot_nt(qs_ref[...], kaug_ref[pl.ds(d0, half), :]))
    p0 = jnp.where(col <= row, p0, 0.0).astype(BF16)
    p1 = jnp.exp2(_dot_nt(qs_ref[tq:2 * tq, :], kaug_ref[pl.ds(d1, half), :]))
    p1 = jnp.where(col + half <= row[tq:2 * tq], p1, 0.0).astype(BF16)
    acc_ref[...] += _dot(p0, vaug_ref[pl.ds(d0, half), :])
    acc_ref[tq:2 * tq, :] += _dot(p1, vaug_ref[pl.ds(d1, half), :])

    for part in range(2):
        base = part * tq
        a1 = acc_ref[base:base + half, :]
        a2 = acc_ref[base + half:base + tq, :]
        o_ref[part * half:(part + 1) * half, :] = _diff_finish(
            a1[:, 0:LANES] / a1[:, LANES:2 * LANES], a2[:, 0:LANES] / a2[:, LANES:2 * LANES],
            lam_init, lp_ref, g_ref)


def _diff_online_kernel(scal_ref, q_ref, k_ref, v_ref, lp_ref, g_ref, o_ref,
                        qs_ref, m_ref, l_ref, acc_ref, *, tq):
    h = pl.program_id(1)
    i = pl.program_id(2)
    lam_init = scal_ref[0]
    slope2 = scal_ref[2 + h]

    q1, q2 = _stack_q(q_ref[...], tq)
    qs_ref[0:tq, :] = q1
    qs_ref[tq:2 * tq, :] = q2
    m_ref[...] = jnp.full(m_ref.shape, NEG, F32)
    l_ref[...] = jnp.zeros(l_ref.shape, F32)
    acc_ref[...] = jnp.zeros(acc_ref.shape, F32)

    def step(j, masked):
        start = pl.multiple_of(j * tq, tq)
        s = _dot_nt(qs_ref[...], k_ref[pl.ds(start, tq), :])
        col = start + lax.broadcasted_iota(jnp.int32, (1, tq), 1)
        s = s + slope2 * col.astype(F32)
        if masked:
            s = jnp.where(_causal_keep(i, tq, start, tq), s, NEG)
        m_prev = m_ref[...]
        m_new = jnp.maximum(m_prev, jnp.max(s, axis=-1, keepdims=True))
        alpha = jnp.exp2(m_prev - m_new)
        p = jnp.exp2(s - m_new)
        l_ref[...] = alpha * l_ref[...] + jnp.sum(p, axis=-1, keepdims=True)
        acc_ref[...] = alpha * acc_ref[...] + _dot(p.astype(BF16), v_ref[pl.ds(start, tq), :])
        m_ref[...] = m_new

    def body(j, carry):
        step(j, False)
        return carry

    lax.fori_loop(0, i, body, 0)
    step(i, True)

    acc = acc_ref[...]
    l = l_ref[...]
    o_ref[...] = _diff_finish(acc[0:tq] / l[0:tq], acc[tq:2 * tq] / l[tq:2 * tq],
                              lam_init, lp_ref, g_ref)


def _diff(qd, kd, vd, scal, lp, gsub, *, bsz, seq, tq, fixed_reference):
    nq = seq // tq
    t = bsz * seq
    if fixed_reference:
        kernel = functools.partial(_diff_fixed_kernel, tq=tq, seq=seq)
        scratch = [
            pltpu.VMEM((seq, 2 * LANES), BF16),
            pltpu.VMEM((seq, 2 * LANES), BF16),
            pltpu.VMEM((2 * tq, 2 * LANES), BF16),
            pltpu.VMEM((2 * tq, 2 * LANES), F32),
            pltpu.VMEM((2, tq // 2, LANES), F32),
        ]
        scratch_bytes = (2 * seq * 2 * LANES * 2 + 2 * tq * 2 * LANES * 6 + tq * LANES * 4
                         + 2 * tq * tq * 6)
        name = "diff"
    else:
        kernel = functools.partial(_diff_online_kernel, tq=tq)
        scratch = [
            pltpu.VMEM((2 * tq, LANES), BF16),
            pltpu.VMEM((2 * tq, 1), F32),
            pltpu.VMEM((2 * tq, 1), F32),
            pltpu.VMEM((2 * tq, LANES), F32),
        ]
        scratch_bytes = 2 * tq * (LANES * 6 + 8) + 6 * 2 * tq * tq * 4
        name = "diff_online"
    nbytes = 2 * (2 * seq * LANES * 2 + 2 * tq * LANES * 2) + scratch_bytes
    return pl.pallas_call(
        kernel,
        grid=(bsz, DIFF_HEADS, nq),
        in_specs=[
            pl.BlockSpec(memory_space=pltpu.SMEM),
            pl.BlockSpec((None, tq, LANES), lambda b, h, i: (h, b * nq + i, 0)),
            pl.BlockSpec((None, seq, LANES), lambda b, h, i: (h, b, 0)),
            pl.BlockSpec((None, seq, LANES), lambda b, h, i: (h, b, 0)),
            pl.BlockSpec((4, HEAD_DIM), lambda b, h, i: (0, 0)),
            pl.BlockSpec((1, DIFF_V_DIM), lambda b, h, i: (0, 0)),
        ],
        out_specs=pl.BlockSpec((tq, LANES), lambda b, h, i: (b * nq + i, h)),
        out_shape=jax.ShapeDtypeStruct((t, DIFF_HEADS * DIFF_V_DIM), BF16),
        scratch_shapes=scratch,
        compiler_params=pltpu.CompilerParams(
            dimension_semantics=("arbitrary", "arbitrary", "arbitrary"),
            vmem_limit_bytes=_vmem_limit(nbytes)),
        name=name,
    )(scal, qd, kd, vd, lp, gsub)


def _mixffn_kernel(x_ref, oa_ref, ob_ref, gate_ref, wb_ref, wo_ref, gn_ref, win_ref, wout_ref,
                   out_ref, h_ref, *, chunk):
    d = x_ref.shape[1]
    up_a = _dot(oa_ref[...], wb_ref[0])
    up_b = _dot(ob_ref[...], wb_ref[1])
    merged = gate_ref[:, :d].astype(F32) * up_a + gate_ref[:, d:].astype(F32) * up_b
    x = x_ref[...] + _dot(merged.astype(BF16), wo_ref[...])
    ms = jnp.mean(x * x, axis=-1, keepdims=True)
    h_ref[...] = (x * lax.rsqrt(ms + EPS) * gn_ref[...]).astype(BF16)
    hidden = wout_ref.shape[0]
    acc = x
    for c in range(hidden // chunk):
        g = _dot(h_ref[...], win_ref[:, c * chunk:(c + 1) * chunk])
        u = _dot(h_ref[...], win_ref[:, hidden + c * chunk:hidden + (c + 1) * chunk])
        a = (g * jax.nn.sigmoid(g) * u).astype(BF16)
        acc = acc + _dot(a, wout_ref[c * chunk:(c + 1) * chunk, :])
    out_ref[...] = acc


def _mixffn(x, oa, ob, gates, wb, wo, gn, win, wout, *, tm, chunk):
    t, d = x.shape
    hidden = wout.shape[0]
    row = lambda i: (i, 0)
    const = lambda i: (0, 0)
    resident = dict(pipeline_mode=pl.Buffered(1))
    kernel = functools.partial(_mixffn_kernel, chunk=chunk)
    weight_bytes = (2 * 512 * d + d * d + 3 * d * hidden) * 2
    nbytes = weight_bytes + 2 * (2 * tm * d * 4 + 2 * tm * 512 * 2 + tm * 2 * d * 2) + tm * d * 10
    return pl.pallas_call(
        kernel,
        grid=(t // tm,),
        in_specs=[
            pl.BlockSpec((tm, d), row),
            pl.BlockSpec((tm, 512), row),
            pl.BlockSpec((tm, 512), row),
            pl.BlockSpec((tm, 2 * d), row),
            pl.BlockSpec(wb.shape, lambda i: (0, 0, 0), **resident),
            pl.BlockSpec(wo.shape, const, **resident),
            pl.BlockSpec((1, d), const),
            pl.BlockSpec((d, 2 * hidden), const, **resident),
            pl.BlockSpec((hidden, d), const, **resident),
        ],
        out_specs=pl.BlockSpec((tm, d), row),
        out_shape=jax.ShapeDtypeStruct(x.shape, F32),
        scratch_shapes=[pltpu.VMEM((tm, d), BF16)],
        compiler_params=pltpu.CompilerParams(
            dimension_semantics=("arbitrary",), vmem_limit_bytes=_vmem_limit(nbytes)),
        name="mixffn",
    )(x, oa, ob, gates, wb, wo, gn, win, wout)


def _alibi_slopes():
    return jnp.exp2(-8.0 * jnp.arange(1, N_ALIBI_HEADS + 1, dtype=F32) / N_ALIBI_HEADS)


def _group_mean_matrix():
    idx = jnp.arange(256) // HEAD_DIM
    return jnp.where(idx[:, None] == idx[None, :], 1.0 / HEAD_DIM, 0.0).astype(BF16)


def kernel(x, w_in, b_gate, w_branch, w_o, norm_mix, norm_ffn, qk_norm_swa, qk_norm_diff,
           attn_sinks, diff_lambda, diff_subln, w_ffn_in, w_ffn_out):
    bsz, seq, d = x.shape
    depth = w_in.shape[0]
    t = bsz * seq
    slopes = _alibi_slopes()
    slopes_swa, slopes_diff = slopes[:SWA_Q_HEADS], slopes[SWA_Q_HEADS:]
    bd = _group_mean_matrix()
    ones128 = jnp.ones((128,), F32)

    xf = x.reshape(t, d)
    for l in range(depth):
        gq_a = jnp.tile(qk_norm_swa[l, 0], SWA_Q_HEADS) * (QK_SCALE * LOG2E)
        gk_a = jnp.tile(qk_norm_swa[l, 1], SWA_KV_HEADS)
        gq_d = jnp.tile(qk_norm_diff[l, 0], 2 * DIFF_HEADS) * (QK_SCALE * LOG2E)
        gk_d = jnp.tile(qk_norm_diff[l, 1], 2 * DIFF_HEADS)
        gcol = jnp.concatenate([gq_a, gq_d, gk_d, gk_a, ones128]).reshape(1, -1).astype(F32)
        wl = w_in[l]
        w_perm = jnp.concatenate(
            [wl[:, 0:512], wl[:, 768:1792], wl[:, 512:768], wl[:, 1792:]], axis=1).astype(BF16)

        qa, ka, va, qd, kd, vd, gates = _proj(
            xf, norm_mix[l].reshape(1, d), w_perm, gcol, bd,
            b_gate[l].reshape(1, 2 * d), tm=512)

        bound_a = (SCORE_BOUND_PER_GAIN * jnp.max(jnp.abs(qk_norm_swa[l, 0]))
                   * jnp.max(jnp.abs(qk_norm_swa[l, 1]))).astype(F32)
        sinks2 = attn_sinks[l].astype(F32) * LOG2E
        ref_a = jnp.maximum(bound_a * LOG2E, sinks2)
        scal_a = jnp.concatenate([slopes_swa * LOG2E, sinks2, ref_a, jnp.exp2(sinks2 - ref_a)]).astype(F32)
        o_a = lax.cond(
            bound_a <= FIXED_REF_MAX_BOUND,
            lambda a: _swa(*a, bsz=bsz, seq=seq, tq=512, fixed_reference=True),
            lambda a: _swa(*a, bsz=bsz, seq=seq, tq=512, fixed_reference=False),
            (qa, ka, va, scal_a))

        lam_init = 0.8 - 0.6 * math.exp(-0.3 * l)
        bound = (SCORE_BOUND_PER_GAIN * jnp.max(jnp.abs(qk_norm_diff[l, 0]))
                 * jnp.max(jnp.abs(qk_norm_diff[l, 1]))).astype(F32)
        scal = jnp.concatenate([jnp.array([lam_init], F32), (bound * LOG2E).reshape(1),
                                slopes_diff * LOG2E]).astype(F32)
        diff_args = (qd, kd, vd, scal, diff_lambda[l].astype(F32), diff_subln[l].reshape(1, DIFF_V_DIM))
        o_b = lax.cond(
            bound <= FIXED_REF_MAX_BOUND,
            lambda a: _diff(*a, bsz=bsz, seq=seq, tq=1024, fixed_reference=True),
            lambda a: _diff(*a, bsz=bsz, seq=seq, tq=256, fixed_reference=False),
            diff_args)

        xf = _mixffn(xf, o_a, o_b, gates, w_branch[l].astype(BF16), w_o[l].astype(BF16),
                     norm_ffn[l].reshape(1, d), w_ffn_in[l].astype(BF16),
                     w_ffn_out[l].astype(BF16), tm=512, chunk=256)
    return xf.reshape(bsz, seq, d)
```

```python
import functools
import math

import jax
import jax.numpy as jnp
from jax import lax
from jax.experimental import pallas as pl
from jax.experimental.pallas import tpu as pltpu

F32 = jnp.float32
BF16 = jnp.bfloat16

HEAD_DIM = 64
BLOCK = 128
SWA_Q_HEADS = 8
SWA_KV_HEADS = 2
SWA_GROUP = SWA_Q_HEADS // SWA_KV_HEADS
DIFF_HEADS = 4
DIFF_V_DIM = 2 * HEAD_DIM
N_ALIBI_HEADS = SWA_Q_HEADS + DIFF_HEADS
NEG = -1e30
EPS = 1e-6
QK_SCALE = HEAD_DIM ** -0.5
LOG2E = math.log2(math.e)
SCORE_BOUND_PER_GAIN = HEAD_DIM * QK_SCALE
FIXED_REF_MAX_BOUND = 20.0

V7X_VMEM_BYTES = 64 * 1024 * 1024
LANES = 128


def _vmem_limit(nbytes):
    return int(min(max(2 * nbytes, 32 * 1024 * 1024), V7X_VMEM_BYTES - 8 * 1024 * 1024))


def _dot(a, b):
    return jnp.dot(a, b, preferred_element_type=F32)


def _dot_nt(a, b):
    return lax.dot_general(a, b, (((1,), (1,)), ((), ())), preferred_element_type=F32)


PROJ_CHUNK = 256
PROJ_NORMED = 1792


def _proj_kernel(x_ref, gmix_ref, w_ref, gcol_ref, bd_ref, bg_ref,
                 qa_ref, ka_ref, va_ref, qd_ref, kd_ref, vd_ref, gate_ref, h_ref, y_ref):
    tm = x_ref.shape[0]
    x = x_ref[...]
    ms = jnp.mean(x * x, axis=-1, keepdims=True)
    h_ref[...] = (x * lax.rsqrt(ms + EPS) * gmix_ref[...]).astype(BF16)

    def proj(c0, width):
        return _dot(h_ref[...], w_ref[:, c0:c0 + width])

    nchunk = PROJ_NORMED // PROJ_CHUNK
    y_ref[...] = proj(0, PROJ_NORMED)
    sq = jnp.concatenate(
        [jnp.square(y_ref[:, PROJ_CHUNK * c:PROJ_CHUNK * (c + 1)]).astype(BF16)
         for c in range(nchunk)], axis=0)
    inv = lax.rsqrt(_dot(sq, bd_ref[...]) + EPS)

    def normed(c):
        cs = slice(PROJ_CHUNK * c, PROJ_CHUNK * (c + 1))
        return (y_ref[:, cs] * inv[c * tm:(c + 1) * tm] * gcol_ref[:, cs]).astype(BF16)

    for c in range(2):
        qa_ref[:, PROJ_CHUNK * c:PROJ_CHUNK * (c + 1)] = normed(c)
    ka_ref[...] = normed(2)[:, :LANES]
    va_ref[...] = y_ref[:, 2 * PROJ_CHUNK + LANES:3 * PROJ_CHUNK].astype(BF16)
    for c in range(2):
        y = normed(3 + c)
        qd_ref[2 * c] = y[:, :LANES]
        qd_ref[2 * c + 1] = y[:, LANES:]
    for c in range(2):
        y = normed(5 + c)
        kd_ref[2 * c] = y[:, :LANES]
        kd_ref[2 * c + 1] = y[:, LANES:]
    y = proj(PROJ_NORMED, DIFF_HEADS * LANES).astype(BF16)
    for hd in range(DIFF_HEADS):
        vd_ref[hd] = y[:, hd * LANES:(hd + 1) * LANES]
    g0 = PROJ_NORMED + DIFF_HEADS * LANES
    gate_ref[...] = jax.nn.sigmoid(proj(g0, gate_ref.shape[1]) + bg_ref[...]).astype(BF16)


def _proj(x, gmix, w, gcol, bd, bg, *, tm):
    t, d = x.shape
    ncols = w.shape[1]
    grid = (t // tm,)
    const = lambda i: (0, 0)
    out_shape = (
        jax.ShapeDtypeStruct((t, 512), BF16),
        jax.ShapeDtypeStruct((t, 128), BF16),
        jax.ShapeDtypeStruct((t, 128), BF16),
        jax.ShapeDtypeStruct((DIFF_HEADS, t, 128), BF16),
        jax.ShapeDtypeStruct((DIFF_HEADS, t, 128), BF16),
        jax.ShapeDtypeStruct((DIFF_HEADS, t, 128), BF16),
        jax.ShapeDtypeStruct((t, 2 * d), BF16),
    )
    row = lambda i: (i, 0)
    hrow = lambda i: (0, i, 0)
    out_specs = (
        pl.BlockSpec((tm, 512), row),
        pl.BlockSpec((tm, 128), row),
        pl.BlockSpec((tm, 128), row),
        pl.BlockSpec((DIFF_HEADS, tm, 128), hrow),
        pl.BlockSpec((DIFF_HEADS, tm, 128), hrow),
        pl.BlockSpec((DIFF_HEADS, tm, 128), hrow),
        pl.BlockSpec((tm, 2 * d), row),
    )
    in_specs = [
        pl.BlockSpec((tm, d), row),
        pl.BlockSpec((1, d), const),
        pl.BlockSpec((d, ncols), const, pipeline_mode=pl.Buffered(1)),
        pl.BlockSpec((1, gcol.shape[1]), const),
        pl.BlockSpec(bd.shape, const),
        pl.BlockSpec((1, 2 * d), const),
    ]
    nbytes = (2 * tm * d * 4 + d * ncols * 2 + 2 * tm * ncols * 2 + tm * d * 2
              + 3 * tm * PROJ_NORMED * 4)
    return pl.pallas_call(
        _proj_kernel,
        grid=grid,
        in_specs=in_specs,
        out_specs=out_specs,
        out_shape=out_shape,
        scratch_shapes=[pltpu.VMEM((tm, d), BF16), pltpu.VMEM((tm, PROJ_NORMED), F32)],
        compiler_params=pltpu.CompilerParams(
            dimension_semantics=("arbitrary",), vmem_limit_bytes=_vmem_limit(nbytes)),
        name="proj",
    )(x, gmix, w, gcol, bd, bg)


SWA_FILL = 512
MASKED_LOGIT_SHIFT = 1e30


def _swa_fixed_kernel(scal_ref, q_ref, k_ref, v_ref, o_ref, kbuf_ref, vbuf_ref, tab_ref,
                      *, tq, seq):
    b = pl.program_id(0)
    i = pl.program_id(1)
    nblk = tq // BLOCK
    rows = SWA_GROUP * BLOCK
    win = 2 * BLOCK

    @pl.when(jnp.logical_and(b == 0, i == 0))
    def _build_tables():
        row = lax.broadcasted_iota(jnp.int32, (rows, win), 0)
        col = lax.broadcasted_iota(jnp.int32, (rows, win), 1)
        head = lax.shift_right_logical(row, 7)
        dist = jnp.bitwise_and(row, BLOCK - 1) + BLOCK - col
        valid = jnp.logical_and(dist >= 0, dist < BLOCK)
        has_key = jnp.logical_and(valid, col >= BLOCK)
        distf = dist.astype(F32)
        for g in range(SWA_KV_HEADS):
            def per_head(base):
                h0 = SWA_GROUP * g
                return jnp.where(head == 0, scal_ref[base + h0],
                                 jnp.where(head == 1, scal_ref[base + h0 + 1],
                                           jnp.where(head == 2, scal_ref[base + h0 + 2],
                                                     scal_ref[base + h0 + 3])))
            t = per_head(0) * distf + per_head(2 * SWA_Q_HEADS)
            tab_ref[g] = jnp.where(valid, t, MASKED_LOGIT_SHIFT)
            tab_ref[SWA_KV_HEADS + g] = jnp.where(has_key, t, MASKED_LOGIT_SHIFT)

    @pl.when(i == 0)
    def _stage_kv():
        kbuf_ref[:, 0:BLOCK, :] = jnp.zeros((SWA_KV_HEADS, BLOCK, LANES), BF16)
        vbuf_ref[:, 0:BLOCK, :] = jnp.zeros((SWA_KV_HEADS, BLOCK, 2 * LANES), BF16)
        lane = lax.broadcasted_iota(jnp.int32, (SWA_FILL, LANES), 1)
        first = lane < HEAD_DIM

        def body(c, carry):
            r0 = pl.multiple_of(c * SWA_FILL, SWA_FILL)
            dst = pl.ds(pl.multiple_of(r0 + BLOCK, BLOCK), SWA_FILL)
            for src_ref, dst_ref in ((k_ref, kbuf_ref), (v_ref, vbuf_ref)):
                x = src_ref[pl.ds(r0, SWA_FILL), :].astype(F32)
                xr = pltpu.roll(x, HEAD_DIM, 1)
                dst_ref[0, dst, 0:LANES] = jnp.where(first, x, xr).astype(BF16)
                dst_ref[1, dst, 0:LANES] = jnp.where(first, xr, x).astype(BF16)
            vbuf_ref[:, dst, LANES:2 * LANES] = jnp.ones((SWA_KV_HEADS, SWA_FILL, LANES), BF16)
            return carry
        lax.fori_loop(0, seq // SWA_FILL, body, 0)

    lane = lax.broadcasted_iota(jnp.int32, (BLOCK, LANES), 1)
    first = lane < HEAD_DIM
    for r in range(nblk):
        n = i * nblk + r
        w0 = pl.multiple_of(n * BLOCK, BLOCK)
        tsel = jnp.where(n == 0, SWA_KV_HEADS, 0)
        rs = slice(r * BLOCK, (r + 1) * BLOCK)
        for g in range(SWA_KV_HEADS):
            slabs = [q_ref[rs, (2 * g + a) * LANES:(2 * g + a + 1) * LANES] for a in range(2)]
            zero = jnp.zeros_like(slabs[0])
            lhs = jnp.concatenate(
                [jnp.where(first, slabs[0], zero), jnp.where(first, zero, slabs[0]),
                 jnp.where(first, slabs[1], zero), jnp.where(first, zero, slabs[1])], axis=0)
            s = _dot_nt(lhs, kbuf_ref[g, pl.ds(w0, win), :])
            p = jnp.exp2(s - tab_ref[tsel + g]).astype(BF16)
            out = _dot(p, vbuf_ref[g, pl.ds(w0, win), :])
            for a in range(2):
                halves = []
                for k in range(2):
                    hq = SWA_GROUP * g + 2 * a + k
                    blk = out[(2 * a + k) * BLOCK:(2 * a + k + 1) * BLOCK]
                    denom = blk[:, LANES:2 * LANES] + scal_ref[3 * SWA_Q_HEADS + hq]
                    halves.append(blk[:, 0:LANES] / denom)
                o_ref[rs, (2 * g + a) * LANES:(2 * g + a + 1) * LANES] = (
                    jnp.where(first, halves[0], halves[1]).astype(BF16))


def _swa_online_kernel(scal_ref, q_ref, k_ref, v_ref, o_ref, *, tq):
    i = pl.program_id(1)
    qi = lax.broadcasted_iota(jnp.int32, (BLOCK, BLOCK), 0)
    kj = lax.broadcasted_iota(jnp.int32, (BLOCK, BLOCK), 1)
    lower = kj <= qi
    dist = jnp.where(lower, qi - kj, BLOCK + qi - kj).astype(F32)
    for r in range(tq // BLOCK):
        n = i * (tq // BLOCK) + r
        cur = pl.multiple_of(n * BLOCK, BLOCK)
        prev = pl.multiple_of(jnp.maximum(n - 1, 0) * BLOCK, BLOCK)
        valid = jnp.logical_or(lower, n > 0)
        k_cur = k_ref[pl.ds(cur, BLOCK), :]
        k_prev = k_ref[pl.ds(prev, BLOCK), :]
        v_cur = v_ref[pl.ds(cur, BLOCK), :]
        v_prev = v_ref[pl.ds(prev, BLOCK), :]
        outs = []
        for hq in range(SWA_Q_HEADS):
            g = hq // SWA_GROUP
            gs = slice(g * HEAD_DIM, (g + 1) * HEAD_DIM)
            qh = q_ref[r * BLOCK:(r + 1) * BLOCK, hq * HEAD_DIM:(hq + 1) * HEAD_DIM]
            s_cur = _dot_nt(qh, k_cur[:, gs])
            s_prev = _dot_nt(qh, k_prev[:, gs])
            s = jnp.where(lower, s_cur, s_prev) - scal_ref[hq] * dist
            s = jnp.where(valid, s, NEG)
            sink = scal_ref[SWA_Q_HEADS + hq]
            m = jnp.maximum(jnp.max(s, axis=-1, keepdims=True), sink)
            e = jnp.exp2(s - m)
            denom = jnp.sum(e, axis=-1, keepdims=True) + jnp.exp2(sink - m)
            p = e / denom
            p_cur = jnp.where(lower, p, 0.0).astype(BF16)
            p_prev = jnp.where(lower, 0.0, p).astype(BF16)
            outs.append(_dot(p_cur, v_cur[:, gs]) + _dot(p_prev, v_prev[:, gs]))
        o_ref[r * BLOCK:(r + 1) * BLOCK, :] = jnp.concatenate(outs, axis=-1).astype(BF16)


def _swa(qa, ka, va, scal, *, bsz, seq, tq, fixed_reference):
    nq = seq // tq
    nbytes = 2 * (2 * tq * 512 * 2 + 2 * seq * LANES * 2)
    if fixed_reference:
        kernel = functools.partial(_swa_fixed_kernel, tq=tq, seq=seq)
        rows = SWA_GROUP * BLOCK
        scratch = [
            pltpu.VMEM((SWA_KV_HEADS, seq + BLOCK, LANES), BF16),
            pltpu.VMEM((SWA_KV_HEADS, seq + BLOCK, 2 * LANES), BF16),
            pltpu.VMEM((2 * SWA_KV_HEADS, rows, 2 * BLOCK), F32),
        ]
        nbytes += (SWA_KV_HEADS * (seq + BLOCK) * 3 * LANES * 2 + 2 * SWA_KV_HEADS * rows * 2 * BLOCK * 4
                   + 8 * rows * 2 * BLOCK * 4)
        name = "swa"
    else:
        kernel = functools.partial(_swa_online_kernel, tq=tq)
        scratch = []
        name = "swa_online"
    return pl.pallas_call(
        kernel,
        grid=(bsz, nq),
        in_specs=[
            pl.BlockSpec(memory_space=pltpu.SMEM),
            pl.BlockSpec((tq, 512), lambda b, i: (b * nq + i, 0)),
            pl.BlockSpec((seq, LANES), lambda b, i: (b, 0)),
            pl.BlockSpec((seq, LANES), lambda b, i: (b, 0)),
        ],
        out_specs=pl.BlockSpec((tq, 512), lambda b, i: (b * nq + i, 0)),
        out_shape=jax.ShapeDtypeStruct(qa.shape, BF16),
        scratch_shapes=scratch,
        compiler_params=pltpu.CompilerParams(
            dimension_semantics=("arbitrary", "arbitrary"), vmem_limit_bytes=_vmem_limit(nbytes)),
        name=name,
    )(scal, qa, ka, va)


def _stack_q(q, tq):
    lane = lax.broadcasted_iota(jnp.int32, (tq, LANES), 1)
    zero = jnp.zeros_like(q)
    return jnp.where(lane < HEAD_DIM, q, zero), jnp.where(lane < HEAD_DIM, zero, q)


def _diff_finish(o1, o2, lam_init, lp_ref, g_ref):
    lp = lp_ref[...]
    lam = (jnp.exp(jnp.sum(lp[0:1] * lp[1:2], axis=-1, keepdims=True))
           - jnp.exp(jnp.sum(lp[2:3] * lp[3:4], axis=-1, keepdims=True)) + lam_init)
    o = o1 - lam * o2
    ms = jnp.mean(o * o, axis=-1, keepdims=True)
    return (o * lax.rsqrt(ms + EPS) * g_ref[...] * (1.0 - lam_init)).astype(BF16)


def _causal_keep(i, tq, start, width):
    col = start + lax.broadcasted_iota(jnp.int32, (1, width), 1)
    row = i * tq + lax.rem(lax.broadcasted_iota(jnp.int32, (2 * tq, 1), 0), tq)
    return col <= row


AUG_K_BASE = 3
AUG_Q_BASE = 9
DIFF_PART_SHIFT = 9
DIFF_PART = 1 << DIFF_PART_SHIFT
DIFF_KEY_BLOCK = 1024


def _split3(val):
    hi = val.astype(BF16).astype(F32)
    r1 = val - hi
    mid = r1.astype(BF16).astype(F32)
    lo = (r1 - mid).astype(BF16).astype(F32)
    return hi, mid, lo


def _aug_templates(slope2, rows, tmpl_ref):
    r = lax.broadcasted_iota(jnp.int32, (rows, LANES), 0).astype(F32)
    hi, mid, lo = _split3(slope2 * r)
    lane = lax.broadcasted_iota(jnp.int32, (rows, LANES), 1)
    parts = jnp.where(jnp.logical_or(lane == 0, lane == 6), hi,
                      jnp.where(jnp.logical_or(lane == 1, lane == 7), mid, lo))
    k_ones = jnp.logical_and(lane >= 6, lane < 12)
    tmpl_ref[0] = jnp.where(lane < 3, parts, jnp.where(k_ones, 1.0, 0.0))
    tmpl_ref[1] = jnp.where(lane < 6, 1.0, jnp.where(lane < 9, -parts, 0.0))


def _with_scalar_lanes(template, value, base):
    hi, mid, lo = _split3(jnp.full((1, LANES), value, F32))
    lane = lax.broadcasted_iota(jnp.int32, (1, LANES), 1)
    terms = jnp.where(lane == base, hi, jnp.where(lane == base + 1, mid, lo))
    here = jnp.logical_and(lane >= base, lane < base + 3)
    return jnp.where(here, terms, template).astype(BF16)


def _diff_fixed_kernel(scal_ref, q_ref, k_ref, v_ref, lp_ref, g_ref, o_ref,
                       kaug_ref, vaug_ref, qs_ref, acc_ref, tmpl_ref, *, tq, seq):
    h = pl.program_id(1)
    i = pl.program_id(2)
    lam_init = scal_ref[0]
    bound2 = scal_ref[1]
    slope2 = scal_ref[2 + h]
    part = DIFF_PART
    nparts = tq // part

    @pl.when(i == 0)
    def _build_kv():
        _aug_templates(slope2, part, tmpl_ref)

        def body(c, carry):
            r0 = pl.multiple_of(c * part, part)
            kaug_ref[pl.ds(r0, part), 0:LANES] = k_ref[pl.ds(r0, part), :]
            kaug_ref[pl.ds(r0, part), LANES:2 * LANES] = _with_scalar_lanes(
                tmpl_ref[0], slope2 * r0.astype(F32), AUG_K_BASE)
            vaug_ref[pl.ds(r0, part), 0:LANES] = v_ref[pl.ds(r0, part), :]
            vaug_ref[pl.ds(r0, part), LANES:2 * LANES] = jnp.ones((part, LANES), BF16)
            return carry
        lax.fori_loop(0, seq // part, body, 0)

    for c in range(nparts):
        q1, q2 = _stack_q(q_ref[c * part:(c + 1) * part, :], part)
        row0 = (i * tq + c * part).astype(F32)
        aug = _with_scalar_lanes(tmpl_ref[1], -(slope2 * row0 + bound2), AUG_Q_BASE)
        base = 2 * part * c
        qs_ref[base:base + part, 0:LANES] = q1
        qs_ref[base + part:base + 2 * part, 0:LANES] = q2
        qs_ref[base:base + part, LANES:2 * LANES] = aug
        qs_ref[base + part:base + 2 * part, LANES:2 * LANES] = aug
    acc_ref[...] = jnp.zeros(acc_ref.shape, F32)

    def full_block(j):
        start = pl.multiple_of(j * DIFF_KEY_BLOCK, DIFF_KEY_BLOCK)
        p = jnp.exp2(_dot_nt(qs_ref[...], kaug_ref[pl.ds(start, DIFF_KEY_BLOCK), :]))
        acc_ref[...] += _dot(p.astype(BF16), vaug_ref[pl.ds(start, DIFF_KEY_BLOCK), :])

    def block_pair(j, carry):
        full_block(2 * j)
        full_block(2 * j + 1)
        return carry

    lax.fori_loop(0, i * (tq // (2 * DIFF_KEY_BLOCK)), block_pair, 0)

    col = lax.broadcasted_iota(jnp.int32, (1, part), 1)
    for c in range(nparts):
        start = pl.multiple_of(i * tq + c * part, part)
        lo = 2 * part * c
        srow = lax.broadcasted_iota(jnp.int32, (2 * tq - lo, 1), 0)
        row = (c + lax.shift_right_logical(srow, DIFF_PART_SHIFT + 1)) * part + jnp.bitwise_and(srow, part - 1)
        p = jnp.exp2(_dot_nt(qs_ref[lo:2 * tq, :], kaug_ref[pl.ds(start, part), :]))
        p = jnp.where(col + c * part <= row, p, 0.0).astype(BF16)
        acc_ref[lo:2 * tq, :] += _dot(p, vaug_ref[pl.ds(start, part), :])

    for c in range(nparts):
        base = 2 * part * c
        a1 = acc_ref[base:base + part, :]
        a2 = acc_ref[base + part:base + 2 * part, :]
        o_ref[c * part:(c + 1) * part, :] = _diff_finish(
            a1[:, 0:LANES] / a1[:, LANES:2 * LANES], a2[:, 0:LANES] / a2[:, LANES:2 * LANES],
            lam_init, lp_ref, g_ref)


def _diff_online_kernel(scal_ref, q_ref, k_ref, v_ref, lp_ref, g_ref, o_ref,
                        qs_ref, m_ref, l_ref, acc_ref, *, tq):
    h = pl.program_id(1)
    i = pl.program_id(2)
    lam_init = scal_ref[0]
    slope2 = scal_ref[2 + h]

    q1, q2 = _stack_q(q_ref[...], tq)
    qs_ref[0:tq, :] = q1
    qs_ref[tq:2 * tq, :] = q2
    m_ref[...] = jnp.full(m_ref.shape, NEG, F32)
    l_ref[...] = jnp.zeros(l_ref.shape, F32)
    acc_ref[...] = jnp.zeros(acc_ref.shape, F32)

    def step(j, masked):
        start = pl.multiple_of(j * tq, tq)
        s = _dot_nt(qs_ref[...], k_ref[pl.ds(start, tq), :])
        col = start + lax.broadcasted_iota(jnp.int32, (1, tq), 1)
        s = s + slope2 * col.astype(F32)
        if masked:
            s = jnp.where(_causal_keep(i, tq, start, tq), s, NEG)
        m_prev = m_ref[...]
        m_new = jnp.maximum(m_prev, jnp.max(s, axis=-1, keepdims=True))
        alpha = jnp.exp2(m_prev - m_new)
        p = jnp.exp2(s - m_new)
        l_ref[...] = alpha * l_ref[...] + jnp.sum(p, axis=-1, keepdims=True)
        acc_ref[...] = alpha * acc_ref[...] + _dot(p.astype(BF16), v_ref[pl.ds(start, tq), :])
        m_ref[...] = m_new

    def body(j, carry):
        step(j, False)
        return carry

    lax.fori_loop(0, i, body, 0)
    step(i, True)

    acc = acc_ref[...]
    l = l_ref[...]
    o_ref[...] = _diff_finish(acc[0:tq] / l[0:tq], acc[tq:2 * tq] / l[tq:2 * tq],
                              lam_init, lp_ref, g_ref)


def _diff(qd, kd, vd, scal, lp, gsub, *, bsz, seq, tq, fixed_reference):
    nq = seq // tq
    t = bsz * seq
    if fixed_reference:
        assert tq % (2 * DIFF_KEY_BLOCK) == 0 and seq % tq == 0
        kernel = functools.partial(_diff_fixed_kernel, tq=tq, seq=seq)
        scratch = [
            pltpu.VMEM((seq, 2 * LANES), BF16),
            pltpu.VMEM((seq, 2 * LANES), BF16),
            pltpu.VMEM((2 * tq, 2 * LANES), BF16),
            pltpu.VMEM((2 * tq, 2 * LANES), F32),
            pltpu.VMEM((2, DIFF_PART, LANES), F32),
        ]
        scratch_bytes = (2 * seq * 2 * LANES * 2 + 2 * tq * 2 * LANES * 6 + 2 * DIFF_PART * LANES * 4
                         + 2 * tq * DIFF_KEY_BLOCK * 6)
        name = "diff"
    else:
        kernel = functools.partial(_diff_online_kernel, tq=tq)
        scratch = [
            pltpu.VMEM((2 * tq, LANES), BF16),
            pltpu.VMEM((2 * tq, 1), F32),
            pltpu.VMEM((2 * tq, 1), F32),
            pltpu.VMEM((2 * tq, LANES), F32),
        ]
        scratch_bytes = 2 * tq * (LANES * 6 + 8) + 6 * 2 * tq * tq * 4
        name = "diff_online"
    nbytes = 2 * (2 * seq * LANES * 2 + 2 * tq * LANES * 2) + scratch_bytes
    return pl.pallas_call(
        kernel,
        grid=(bsz, DIFF_HEADS, nq),
        in_specs=[
            pl.BlockSpec(memory_space=pltpu.SMEM),
            pl.BlockSpec((None, tq, LANES), lambda b, h, i: (h, b * nq + i, 0)),
            pl.BlockSpec((None, seq, LANES), lambda b, h, i: (h, b, 0)),
            pl.BlockSpec((None, seq, LANES), lambda b, h, i: (h, b, 0)),
            pl.BlockSpec((4, HEAD_DIM), lambda b, h, i: (0, 0)),
            pl.BlockSpec((1, DIFF_V_DIM), lambda b, h, i: (0, 0)),
        ],
        out_specs=pl.BlockSpec((tq, LANES), lambda b, h, i: (b * nq + i, h)),
        out_shape=jax.ShapeDtypeStruct((t, DIFF_HEADS * DIFF_V_DIM), BF16),
        scratch_shapes=scratch,
        compiler_params=pltpu.CompilerParams(
            dimension_semantics=("arbitrary", "arbitrary", "arbitrary"),
            vmem_limit_bytes=_vmem_limit(nbytes)),
        name=name,
    )(scal, qd, kd, vd, lp, gsub)


def _mixffn_kernel(x_ref, oa_ref, ob_ref, gate_ref, wb_ref, wo_ref, gn_ref, win_ref, wout_ref,
                   out_ref, h_ref, *, chunk):
    d = x_ref.shape[1]
    up_a = _dot(oa_ref[...], wb_ref[0])
    up_b = _dot(ob_ref[...], wb_ref[1])
    merged = gate_ref[:, :d].astype(F32) * up_a + gate_ref[:, d:].astype(F32) * up_b
    x = x_ref[...] + _dot(merged.astype(BF16), wo_ref[...])
    ms = jnp.mean(x * x, axis=-1, keepdims=True)
    h_ref[...] = (x * lax.rsqrt(ms + EPS) * gn_ref[...]).astype(BF16)
    hidden = wout_ref.shape[0]
    acc = x
    for c in range(hidden // chunk):
        g = _dot(h_ref[...], win_ref[:, c * chunk:(c + 1) * chunk])
        u = _dot(h_ref[...], win_ref[:, hidden + c * chunk:hidden + (c + 1) * chunk])
        a = (g * jax.nn.sigmoid(g) * u).astype(BF16)
        acc = acc + _dot(a, wout_ref[c * chunk:(c + 1) * chunk, :])
    out_ref[...] = acc


def _mixffn(x, oa, ob, gates, wb, wo, gn, win, wout, *, tm, chunk):
    t, d = x.shape
    hidden = wout.shape[0]
    row = lambda i: (i, 0)
    const = lambda i: (0, 0)
    resident = dict(pipeline_mode=pl.Buffered(1))
    kernel = functools.partial(_mixffn_kernel, chunk=chunk)
    weight_bytes = (2 * 512 * d + d * d + 3 * d * hidden) * 2
    nbytes = weight_bytes + 2 * (2 * tm * d * 4 + 2 * tm * 512 * 2 + tm * 2 * d * 2) + tm * d * 10
    return pl.pallas_call(
        kernel,
        grid=(t // tm,),
        in_specs=[
            pl.BlockSpec((tm, d), row),
            pl.BlockSpec((tm, 512), row),
            pl.BlockSpec((tm, 512), row),
            pl.BlockSpec((tm, 2 * d), row),
            pl.BlockSpec(wb.shape, lambda i: (0, 0, 0), **resident),
            pl.BlockSpec(wo.shape, const, **resident),
            pl.BlockSpec((1, d), const),
            pl.BlockSpec((d, 2 * hidden), const, **resident),
            pl.BlockSpec((hidden, d), const, **resident),
        ],
        out_specs=pl.BlockSpec((tm, d), row),
        out_shape=jax.ShapeDtypeStruct(x.shape, F32),
        scratch_shapes=[pltpu.VMEM((tm, d), BF16)],
        compiler_params=pltpu.CompilerParams(
            dimension_semantics=("arbitrary",), vmem_limit_bytes=_vmem_limit(nbytes)),
        name="mixffn",
    )(x, oa, ob, gates, wb, wo, gn, win, wout)


def _alibi_slopes():
    return jnp.exp2(-8.0 * jnp.arange(1, N_ALIBI_HEADS + 1, dtype=F32) / N_ALIBI_HEADS)


def _group_mean_matrix():
    idx = jnp.arange(256) // HEAD_DIM
    return jnp.where(idx[:, None] == idx[None, :], 1.0 / HEAD_DIM, 0.0).astype(BF16)


def kernel(x, w_in, b_gate, w_branch, w_o, norm_mix, norm_ffn, qk_norm_swa, qk_norm_diff,
           attn_sinks, diff_lambda, diff_subln, w_ffn_in, w_ffn_out):
    bsz, seq, d = x.shape
    depth = w_in.shape[0]
    t = bsz * seq
    slopes = _alibi_slopes()
    slopes_swa, slopes_diff = slopes[:SWA_Q_HEADS], slopes[SWA_Q_HEADS:]
    bd = _group_mean_matrix()
    ones128 = jnp.ones((128,), F32)

    xf = x.reshape(t, d)
    for l in range(depth):
        gq_a = jnp.tile(qk_norm_swa[l, 0], SWA_Q_HEADS) * (QK_SCALE * LOG2E)
        gk_a = jnp.tile(qk_norm_swa[l, 1], SWA_KV_HEADS)
        gq_d = jnp.tile(qk_norm_diff[l, 0], 2 * DIFF_HEADS) * (QK_SCALE * LOG2E)
        gk_d = jnp.tile(qk_norm_diff[l, 1], 2 * DIFF_HEADS)
        gcol = jnp.concatenate([gq_a, gk_a, ones128, gq_d, gk_d]).reshape(1, -1).astype(F32)

        qa, ka, va, qd, kd, vd, gates = _proj(
            xf, norm_mix[l].reshape(1, d), w_in[l].astype(BF16), gcol, bd,
            b_gate[l].reshape(1, 2 * d), tm=512)

        bound_a = (SCORE_BOUND_PER_GAIN * jnp.max(jnp.abs(qk_norm_swa[l, 0]))
                   * jnp.max(jnp.abs(qk_norm_swa[l, 1]))).astype(F32)
        sinks2 = attn_sinks[l].astype(F32) * LOG2E
        ref_a = jnp.maximum(bound_a * LOG2E, sinks2)
        scal_a = jnp.concatenate([slopes_swa * LOG2E, sinks2, ref_a, jnp.exp2(sinks2 - ref_a)]).astype(F32)
        o_a = lax.cond(
            bound_a <= FIXED_REF_MAX_BOUND,
            lambda a: _swa(*a, bsz=bsz, seq=seq, tq=1024, fixed_reference=True),
            lambda a: _swa(*a, bsz=bsz, seq=seq, tq=512, fixed_reference=False),
            (qa, ka, va, scal_a))

        lam_init = 0.8 - 0.6 * math.exp(-0.3 * l)
        bound = (SCORE_BOUND_PER_GAIN * jnp.max(jnp.abs(qk_norm_diff[l, 0]))
                 * jnp.max(jnp.abs(qk_norm_diff[l, 1]))).astype(F32)
        scal = jnp.concatenate([jnp.array([lam_init], F32), (bound * LOG2E).reshape(1),
                                slopes_diff * LOG2E]).astype(F32)
        diff_args = (qd, kd, vd, scal, diff_lambda[l].astype(F32), diff_subln[l].reshape(1, DIFF_V_DIM))
        o_b = lax.cond(
            bound <= FIXED_REF_MAX_BOUND,
            lambda a: _diff(*a, bsz=bsz, seq=seq, tq=2048, fixed_reference=True),
            lambda a: _diff(*a, bsz=bsz, seq=seq, tq=256, fixed_reference=False),
            diff_args)

        xf = _mixffn(xf, o_a, o_b, gates, w_branch[l].astype(BF16), w_o[l].astype(BF16),
                     norm_ffn[l].reshape(1, d), w_ffn_in[l].astype(BF16),
                     w_ffn_out[l].astype(BF16), tm=512, chunk=256)
    return xf.reshape(bsz, seq, d)
```

```python
import functools
import math

import jax
import jax.numpy as jnp
from jax import lax
from jax.experimental import pallas as pl
from jax.experimental.pallas import tpu as pltpu

F32 = jnp.float32
BF16 = jnp.bfloat16

HEAD_DIM = 64
BLOCK = 128
SWA_Q_HEADS = 8
SWA_KV_HEADS = 2
SWA_GROUP = SWA_Q_HEADS // SWA_KV_HEADS
DIFF_HEADS = 4
DIFF_V_DIM = 2 * HEAD_DIM
N_ALIBI_HEADS = SWA_Q_HEADS + DIFF_HEADS
NEG = -1e30
EPS = 1e-6
QK_SCALE = HEAD_DIM ** -0.5
LOG2E = math.log2(math.e)
SCORE_BOUND_PER_GAIN = HEAD_DIM * QK_SCALE
FIXED_REF_MAX_BOUND = 20.0

V7X_VMEM_BYTES = 64 * 1024 * 1024
LANES = 128


def _vmem_limit(nbytes):
    return int(min(max(2 * nbytes, 32 * 1024 * 1024), V7X_VMEM_BYTES - 8 * 1024 * 1024))


def _dot(a, b):
    return jnp.dot(a, b, preferred_element_type=F32)


def _dot_nt(a, b):
    return lax.dot_general(a, b, (((1,), (1,)), ((), ())), preferred_element_type=F32)


PROJ_CHUNK = 256
PROJ_NORMED = 1792


def _proj_kernel(x_ref, gmix_ref, w_ref, gcol_ref, bd_ref, bg_ref,
                 qa_ref, ka_ref, va_ref, qd_ref, kd_ref, vd_ref, gate_ref, h_ref, y_ref):
    tm = x_ref.shape[0]
    x = x_ref[...]
    ms = jnp.mean(x * x, axis=-1, keepdims=True)
    h_ref[...] = (x * lax.rsqrt(ms + EPS) * gmix_ref[...]).astype(BF16)

    def proj(c0, width):
        return _dot(h_ref[...], w_ref[:, c0:c0 + width])

    nchunk = PROJ_NORMED // PROJ_CHUNK
    y_ref[...] = proj(0, PROJ_NORMED)
    sq = jnp.concatenate(
        [jnp.square(y_ref[:, PROJ_CHUNK * c:PROJ_CHUNK * (c + 1)]).astype(BF16)
         for c in range(nchunk)], axis=0)
    inv = lax.rsqrt(_dot(sq, bd_ref[...]) + EPS)

    def normed(c):
        cs = slice(PROJ_CHUNK * c, PROJ_CHUNK * (c + 1))
        return (y_ref[:, cs] * inv[c * tm:(c + 1) * tm] * gcol_ref[:, cs]).astype(BF16)

    for c in range(2):
        qa_ref[:, PROJ_CHUNK * c:PROJ_CHUNK * (c + 1)] = normed(c)
    ka_ref[...] = normed(2)[:, :LANES]
    va_ref[...] = y_ref[:, 2 * PROJ_CHUNK + LANES:3 * PROJ_CHUNK].astype(BF16)
    for c in range(2):
        y = normed(3 + c)
        qd_ref[2 * c] = y[:, :LANES]
        qd_ref[2 * c + 1] = y[:, LANES:]
    for c in range(2):
        y = normed(5 + c)
        kd_ref[2 * c] = y[:, :LANES]
        kd_ref[2 * c + 1] = y[:, LANES:]
    y = proj(PROJ_NORMED, DIFF_HEADS * LANES).astype(BF16)
    for hd in range(DIFF_HEADS):
        vd_ref[hd] = y[:, hd * LANES:(hd + 1) * LANES]
    g0 = PROJ_NORMED + DIFF_HEADS * LANES
    gate_ref[...] = jax.nn.sigmoid(proj(g0, gate_ref.shape[1]) + bg_ref[...]).astype(BF16)


def _proj(x, gmix, w, gcol, bd, bg, *, layer, tm):
    t, d = x.shape
    ncols = w.shape[2]
    grid = (t // tm,)
    const = lambda i: (0, 0)
    out_shape = (
        jax.ShapeDtypeStruct((t, 512), BF16),
        jax.ShapeDtypeStruct((t, 128), BF16),
        jax.ShapeDtypeStruct((t, 128), BF16),
        jax.ShapeDtypeStruct((DIFF_HEADS, t, 128), BF16),
        jax.ShapeDtypeStruct((DIFF_HEADS, t, 128), BF16),
        jax.ShapeDtypeStruct((DIFF_HEADS, t, 128), BF16),
        jax.ShapeDtypeStruct((t, 2 * d), BF16),
    )
    row = lambda i: (i, 0)
    hrow = lambda i: (0, i, 0)
    out_specs = (
        pl.BlockSpec((tm, 512), row),
        pl.BlockSpec((tm, 128), row),
        pl.BlockSpec((tm, 128), row),
        pl.BlockSpec((DIFF_HEADS, tm, 128), hrow),
        pl.BlockSpec((DIFF_HEADS, tm, 128), hrow),
        pl.BlockSpec((DIFF_HEADS, tm, 128), hrow),
        pl.BlockSpec((tm, 2 * d), row),
    )
    lconst = lambda i: (layer, 0, 0)
    in_specs = [
        pl.BlockSpec((tm, d), row),
        pl.BlockSpec((None, 1, d), lconst),
        pl.BlockSpec((None, d, ncols), lconst, pipeline_mode=pl.Buffered(1)),
        pl.BlockSpec((None, 1, gcol.shape[2]), lconst),
        pl.BlockSpec(bd.shape, const),
        pl.BlockSpec((None, 1, 2 * d), lconst),
    ]
    nbytes = (2 * tm * d * 4 + d * ncols * 2 + 2 * tm * ncols * 2 + tm * d * 2
              + 3 * tm * PROJ_NORMED * 4)
    return pl.pallas_call(
        _proj_kernel,
        grid=grid,
        in_specs=in_specs,
        out_specs=out_specs,
        out_shape=out_shape,
        scratch_shapes=[pltpu.VMEM((tm, d), BF16), pltpu.VMEM((tm, PROJ_NORMED), F32)],
        compiler_params=pltpu.CompilerParams(
            dimension_semantics=("arbitrary",), vmem_limit_bytes=_vmem_limit(nbytes)),
        name="proj",
    )(x, gmix, w, gcol, bd, bg)


SWA_FILL = 512
MASKED_LOGIT_SHIFT = 1e30


def _swa_fixed_kernel(scal_ref, q_ref, k_ref, v_ref, o_ref, kbuf_ref, vbuf_ref, tab_ref,
                      *, tq, seq):
    b = pl.program_id(0)
    i = pl.program_id(1)
    nblk = tq // BLOCK
    rows = SWA_GROUP * BLOCK
    win = 2 * BLOCK

    @pl.when(jnp.logical_and(b == 0, i == 0))
    def _build_tables():
        row = lax.broadcasted_iota(jnp.int32, (rows, win), 0)
        col = lax.broadcasted_iota(jnp.int32, (rows, win), 1)
        head = lax.shift_right_logical(row, 7)
        dist = jnp.bitwise_and(row, BLOCK - 1) + BLOCK - col
        valid = jnp.logical_and(dist >= 0, dist < BLOCK)
        has_key = jnp.logical_and(valid, col >= BLOCK)
        distf = dist.astype(F32)
        for g in range(SWA_KV_HEADS):
            def per_head(base):
                h0 = SWA_GROUP * g
                return jnp.where(head == 0, scal_ref[base + h0],
                                 jnp.where(head == 1, scal_ref[base + h0 + 1],
                                           jnp.where(head == 2, scal_ref[base + h0 + 2],
                                                     scal_ref[base + h0 + 3])))
            t = per_head(0) * distf + per_head(2 * SWA_Q_HEADS)
            tab_ref[g] = jnp.where(valid, t, MASKED_LOGIT_SHIFT)
            tab_ref[SWA_KV_HEADS + g] = jnp.where(has_key, t, MASKED_LOGIT_SHIFT)

    @pl.when(i == 0)
    def _stage_kv():
        kbuf_ref[:, 0:BLOCK, :] = jnp.zeros((SWA_KV_HEADS, BLOCK, LANES), BF16)
        vbuf_ref[:, 0:BLOCK, :] = jnp.zeros((SWA_KV_HEADS, BLOCK, 2 * LANES), BF16)
        lane = lax.broadcasted_iota(jnp.int32, (SWA_FILL, LANES), 1)
        first = lane < HEAD_DIM

        def body(c, carry):
            r0 = pl.multiple_of(c * SWA_FILL, SWA_FILL)
            dst = pl.ds(pl.multiple_of(r0 + BLOCK, BLOCK), SWA_FILL)
            for src_ref, dst_ref in ((k_ref, kbuf_ref), (v_ref, vbuf_ref)):
                x = src_ref[pl.ds(r0, SWA_FILL), :].astype(F32)
                xr = pltpu.roll(x, HEAD_DIM, 1)
                dst_ref[0, dst, 0:LANES] = jnp.where(first, x, xr).astype(BF16)
                dst_ref[1, dst, 0:LANES] = jnp.where(first, xr, x).astype(BF16)
            vbuf_ref[:, dst, LANES:2 * LANES] = jnp.ones((SWA_KV_HEADS, SWA_FILL, LANES), BF16)
            return carry
        lax.fori_loop(0, seq // SWA_FILL, body, 0)

    lane = lax.broadcasted_iota(jnp.int32, (BLOCK, LANES), 1)
    first = lane < HEAD_DIM
    for r in range(nblk):
        n = i * nblk + r
        w0 = pl.multiple_of(n * BLOCK, BLOCK)
        tsel = jnp.where(n == 0, SWA_KV_HEADS, 0)
        rs = slice(r * BLOCK, (r + 1) * BLOCK)
        for g in range(SWA_KV_HEADS):
            slabs = [q_ref[rs, (2 * g + a) * LANES:(2 * g + a + 1) * LANES] for a in range(2)]
            zero = jnp.zeros_like(slabs[0])
            lhs = jnp.concatenate(
                [jnp.where(first, slabs[0], zero), jnp.where(first, zero, slabs[0]),
                 jnp.where(first, slabs[1], zero), jnp.where(first, zero, slabs[1])], axis=0)
            s = _dot_nt(lhs, kbuf_ref[g, pl.ds(w0, win), :])
            p = jnp.exp2(s - tab_ref[tsel + g]).astype(BF16)
            out = _dot(p, vbuf_ref[g, pl.ds(w0, win), :])
            for a in range(2):
                halves = []
                for k in range(2):
                    hq = SWA_GROUP * g + 2 * a + k
                    blk = out[(2 * a + k) * BLOCK:(2 * a + k + 1) * BLOCK]
                    denom = blk[:, LANES:2 * LANES] + scal_ref[3 * SWA_Q_HEADS + hq]
                    halves.append(blk[:, 0:LANES] / denom)
                o_ref[rs, (2 * g + a) * LANES:(2 * g + a + 1) * LANES] = (
                    jnp.where(first, halves[0], halves[1]).astype(BF16))


def _swa_online_kernel(scal_ref, q_ref, k_ref, v_ref, o_ref, *, tq):
    i = pl.program_id(1)
    qi = lax.broadcasted_iota(jnp.int32, (BLOCK, BLOCK), 0)
    kj = lax.broadcasted_iota(jnp.int32, (BLOCK, BLOCK), 1)
    lower = kj <= qi
    dist = jnp.where(lower, qi - kj, BLOCK + qi - kj).astype(F32)
    for r in range(tq // BLOCK):
        n = i * (tq // BLOCK) + r
        cur = pl.multiple_of(n * BLOCK, BLOCK)
        prev = pl.multiple_of(jnp.maximum(n - 1, 0) * BLOCK, BLOCK)
        valid = jnp.logical_or(lower, n > 0)
        k_cur = k_ref[pl.ds(cur, BLOCK), :]
        k_prev = k_ref[pl.ds(prev, BLOCK), :]
        v_cur = v_ref[pl.ds(cur, BLOCK), :]
        v_prev = v_ref[pl.ds(prev, BLOCK), :]
        outs = []
        for hq in range(SWA_Q_HEADS):
            g = hq // SWA_GROUP
            gs = slice(g * HEAD_DIM, (g + 1) * HEAD_DIM)
            qh = q_ref[r * BLOCK:(r + 1) * BLOCK, hq * HEAD_DIM:(hq + 1) * HEAD_DIM]
            s_cur = _dot_nt(qh, k_cur[:, gs])
            s_prev = _dot_nt(qh, k_prev[:, gs])
            s = jnp.where(lower, s_cur, s_prev) - scal_ref[hq] * dist
            s = jnp.where(valid, s, NEG)
            sink = scal_ref[SWA_Q_HEADS + hq]
            m = jnp.maximum(jnp.max(s, axis=-1, keepdims=True), sink)
            e = jnp.exp2(s - m)
            denom = jnp.sum(e, axis=-1, keepdims=True) + jnp.exp2(sink - m)
            p = e / denom
            p_cur = jnp.where(lower, p, 0.0).astype(BF16)
            p_prev = jnp.where(lower, 0.0, p).astype(BF16)
            outs.append(_dot(p_cur, v_cur[:, gs]) + _dot(p_prev, v_prev[:, gs]))
        o_ref[r * BLOCK:(r + 1) * BLOCK, :] = jnp.concatenate(outs, axis=-1).astype(BF16)


def _swa(qa, ka, va, scal, *, bsz, seq, tq, fixed_reference):
    nq = seq // tq
    nbytes = 2 * (2 * tq * 512 * 2 + 2 * seq * LANES * 2)
    if fixed_reference:
        kernel = functools.partial(_swa_fixed_kernel, tq=tq, seq=seq)
        rows = SWA_GROUP * BLOCK
        scratch = [
            pltpu.VMEM((SWA_KV_HEADS, seq + BLOCK, LANES), BF16),
            pltpu.VMEM((SWA_KV_HEADS, seq + BLOCK, 2 * LANES), BF16),
            pltpu.VMEM((2 * SWA_KV_HEADS, rows, 2 * BLOCK), F32),
        ]
        nbytes += (SWA_KV_HEADS * (seq + BLOCK) * 3 * LANES * 2 + 2 * SWA_KV_HEADS * rows * 2 * BLOCK * 4
                   + 8 * rows * 2 * BLOCK * 4)
        name = "swa"
    else:
        kernel = functools.partial(_swa_online_kernel, tq=tq)
        scratch = []
        name = "swa_online"
    return pl.pallas_call(
        kernel,
        grid=(bsz, nq),
        in_specs=[
            pl.BlockSpec(memory_space=pltpu.SMEM),
            pl.BlockSpec((tq, 512), lambda b, i: (b * nq + i, 0)),
            pl.BlockSpec((seq, LANES), lambda b, i: (b, 0)),
            pl.BlockSpec((seq, LANES), lambda b, i: (b, 0)),
        ],
        out_specs=pl.BlockSpec((tq, 512), lambda b, i: (b * nq + i, 0)),
        out_shape=jax.ShapeDtypeStruct(qa.shape, BF16),
        scratch_shapes=scratch,
        compiler_params=pltpu.CompilerParams(
            dimension_semantics=("arbitrary", "arbitrary"), vmem_limit_bytes=_vmem_limit(nbytes)),
        name=name,
    )(scal, qa, ka, va)


def _stack_q(q, tq):
    lane = lax.broadcasted_iota(jnp.int32, (tq, LANES), 1)
    zero = jnp.zeros_like(q)
    return jnp.where(lane < HEAD_DIM, q, zero), jnp.where(lane < HEAD_DIM, zero, q)


def _diff_finish(o1, o2, lam_init, lp_ref, g_ref):
    lp = lp_ref[...]
    lam = (jnp.exp(jnp.sum(lp[0:1] * lp[1:2], axis=-1, keepdims=True))
           - jnp.exp(jnp.sum(lp[2:3] * lp[3:4], axis=-1, keepdims=True)) + lam_init)
    o = o1 - lam * o2
    ms = jnp.mean(o * o, axis=-1, keepdims=True)
    return (o * lax.rsqrt(ms + EPS) * g_ref[...] * (1.0 - lam_init)).astype(BF16)


def _causal_keep(i, tq, start, width):
    col = start + lax.broadcasted_iota(jnp.int32, (1, width), 1)
    row = i * tq + lax.rem(lax.broadcasted_iota(jnp.int32, (2 * tq, 1), 0), tq)
    return col <= row


AUG_K_BASE = 3
AUG_Q_BASE = 9
DIFF_PART_SHIFT = 9
DIFF_PART = 1 << DIFF_PART_SHIFT
DIFF_KEY_BLOCK = 1024


def _split3(val):
    hi = val.astype(BF16).astype(F32)
    r1 = val - hi
    mid = r1.astype(BF16).astype(F32)
    lo = (r1 - mid).astype(BF16).astype(F32)
    return hi, mid, lo


def _aug_templates(slope2, rows, tmpl_ref):
    r = lax.broadcasted_iota(jnp.int32, (rows, LANES), 0).astype(F32)
    hi, mid, lo = _split3(slope2 * r)
    lane = lax.broadcasted_iota(jnp.int32, (rows, LANES), 1)
    parts = jnp.where(jnp.logical_or(lane == 0, lane == 6), hi,
                      jnp.where(jnp.logical_or(lane == 1, lane == 7), mid, lo))
    k_ones = jnp.logical_and(lane >= 6, lane < 12)
    tmpl_ref[0] = jnp.where(lane < 3, parts, jnp.where(k_ones, 1.0, 0.0))
    tmpl_ref[1] = jnp.where(lane < 6, 1.0, jnp.where(lane < 9, -parts, 0.0))


def _with_scalar_lanes(template, value, base):
    hi, mid, lo = _split3(jnp.full((1, LANES), value, F32))
    lane = lax.broadcasted_iota(jnp.int32, (1, LANES), 1)
    terms = jnp.where(lane == base, hi, jnp.where(lane == base + 1, mid, lo))
    here = jnp.logical_and(lane >= base, lane < base + 3)
    return jnp.where(here, terms, template).astype(BF16)


def _diff_fixed_kernel(scal_ref, q_ref, k_ref, v_ref, lp_ref, g_ref, o_ref,
                       kaug_ref, vaug_ref, qs_ref, acc_ref, tmpl_ref, *, tq, seq):
    h = pl.program_id(1)
    i = pl.program_id(2)
    lam_init = scal_ref[0]
    bound2 = scal_ref[1]
    slope2 = scal_ref[2 + h]
    part = DIFF_PART
    nparts = tq // part

    @pl.when(i == 0)
    def _build_kv():
        _aug_templates(slope2, part, tmpl_ref)

        def body(c, carry):
            r0 = pl.multiple_of(c * part, part)
            kaug_ref[pl.ds(r0, part), 0:LANES] = k_ref[pl.ds(r0, part), :]
            kaug_ref[pl.ds(r0, part), LANES:2 * LANES] = _with_scalar_lanes(
                tmpl_ref[0], slope2 * r0.astype(F32), AUG_K_BASE)
            vaug_ref[pl.ds(r0, part), 0:LANES] = v_ref[pl.ds(r0, part), :]
            vaug_ref[pl.ds(r0, part), LANES:2 * LANES] = jnp.ones((part, LANES), BF16)
            return carry
        lax.fori_loop(0, seq // part, body, 0)

    for c in range(nparts):
        q1, q2 = _stack_q(q_ref[c * part:(c + 1) * part, :], part)
        row0 = (i * tq + c * part).astype(F32)
        aug = _with_scalar_lanes(tmpl_ref[1], -(slope2 * row0 + bound2), AUG_Q_BASE)
        base = 2 * part * c
        qs_ref[base:base + part, 0:LANES] = q1
        qs_ref[base + part:base + 2 * part, 0:LANES] = q2
        qs_ref[base:base + part, LANES:2 * LANES] = aug
        qs_ref[base + part:base + 2 * part, LANES:2 * LANES] = aug
    acc_ref[...] = jnp.zeros(acc_ref.shape, F32)

    def full_block(j):
        start = pl.multiple_of(j * DIFF_KEY_BLOCK, DIFF_KEY_BLOCK)
        p = jnp.exp2(_dot_nt(qs_ref[...], kaug_ref[pl.ds(start, DIFF_KEY_BLOCK), :]))
        acc_ref[...] += _dot(p.astype(BF16), vaug_ref[pl.ds(start, DIFF_KEY_BLOCK), :])

    def block_pair(j, carry):
        full_block(2 * j)
        full_block(2 * j + 1)
        return carry

    lax.fori_loop(0, i * (tq // (2 * DIFF_KEY_BLOCK)), block_pair, 0)

    col = lax.broadcasted_iota(jnp.int32, (1, part), 1)
    for c in range(nparts):
        start = pl.multiple_of(i * tq + c * part, part)
        lo = 2 * part * c
        srow = lax.broadcasted_iota(jnp.int32, (2 * tq - lo, 1), 0)
        row = (c + lax.shift_right_logical(srow, DIFF_PART_SHIFT + 1)) * part + jnp.bitwise_and(srow, part - 1)
        p = jnp.exp2(_dot_nt(qs_ref[lo:2 * tq, :], kaug_ref[pl.ds(start, part), :]))
        p = jnp.where(col + c * part <= row, p, 0.0).astype(BF16)
        acc_ref[lo:2 * tq, :] += _dot(p, vaug_ref[pl.ds(start, part), :])

    for c in range(nparts):
        base = 2 * part * c
        a1 = acc_ref[base:base + part, :]
        a2 = acc_ref[base + part:base + 2 * part, :]
        o_ref[c * part:(c + 1) * part, :] = _diff_finish(
            a1[:, 0:LANES] / a1[:, LANES:2 * LANES], a2[:, 0:LANES] / a2[:, LANES:2 * LANES],
            lam_init, lp_ref, g_ref)


def _diff_online_kernel(scal_ref, q_ref, k_ref, v_ref, lp_ref, g_ref, o_ref,
                        qs_ref, m_ref, l_ref, acc_ref, *, tq):
    h = pl.program_id(1)
    i = pl.program_id(2)
    lam_init = scal_ref[0]
    slope2 = scal_ref[2 + h]

    q1, q2 = _stack_q(q_ref[...], tq)
    qs_ref[0:tq, :] = q1
    qs_ref[tq:2 * tq, :] = q2
    m_ref[...] = jnp.full(m_ref.shape, NEG, F32)
    l_ref[...] = jnp.zeros(l_ref.shape, F32)
    acc_ref[...] = jnp.zeros(acc_ref.shape, F32)

    def step(j, masked):
        start = pl.multiple_of(j * tq, tq)
        s = _dot_nt(qs_ref[...], k_ref[pl.ds(start, tq), :])
        col = start + lax.broadcasted_iota(jnp.int32, (1, tq), 1)
        s = s + slope2 * col.astype(F32)
        if masked:
            s = jnp.where(_causal_keep(i, tq, start, tq), s, NEG)
        m_prev = m_ref[...]
        m_new = jnp.maximum(m_prev, jnp.max(s, axis=-1, keepdims=True))
        alpha = jnp.exp2(m_prev - m_new)
        p = jnp.exp2(s - m_new)
        l_ref[...] = alpha * l_ref[...] + jnp.sum(p, axis=-1, keepdims=True)
        acc_ref[...] = alpha * acc_ref[...] + _dot(p.astype(BF16), v_ref[pl.ds(start, tq), :])
        m_ref[...] = m_new

    def body(j, carry):
        step(j, False)
        return carry

    lax.fori_loop(0, i, body, 0)
    step(i, True)

    acc = acc_ref[...]
    l = l_ref[...]
    o_ref[...] = _diff_finish(acc[0:tq] / l[0:tq], acc[tq:2 * tq] / l[tq:2 * tq],
                              lam_init, lp_ref, g_ref)


def _diff(qd, kd, vd, scal, lp, gsub, *, layer, bsz, seq, tq, fixed_reference):
    nq = seq // tq
    t = bsz * seq
    if fixed_reference:
        assert tq % (2 * DIFF_KEY_BLOCK) == 0 and seq % tq == 0
        kernel = functools.partial(_diff_fixed_kernel, tq=tq, seq=seq)
        scratch = [
            pltpu.VMEM((seq, 2 * LANES), BF16),
            pltpu.VMEM((seq, 2 * LANES), BF16),
            pltpu.VMEM((2 * tq, 2 * LANES), BF16),
            pltpu.VMEM((2 * tq, 2 * LANES), F32),
            pltpu.VMEM((2, DIFF_PART, LANES), F32),
        ]
        scratch_bytes = (2 * seq * 2 * LANES * 2 + 2 * tq * 2 * LANES * 6 + 2 * DIFF_PART * LANES * 4
                         + 2 * tq * DIFF_KEY_BLOCK * 6)
        name = "diff"
    else:
        kernel = functools.partial(_diff_online_kernel, tq=tq)
        scratch = [
            pltpu.VMEM((2 * tq, LANES), BF16),
            pltpu.VMEM((2 * tq, 1), F32),
            pltpu.VMEM((2 * tq, 1), F32),
            pltpu.VMEM((2 * tq, LANES), F32),
        ]
        scratch_bytes = 2 * tq * (LANES * 6 + 8) + 6 * 2 * tq * tq * 4
        name = "diff_online"
    nbytes = 2 * (2 * seq * LANES * 2 + 2 * tq * LANES * 2) + scratch_bytes
    return pl.pallas_call(
        kernel,
        grid=(bsz, DIFF_HEADS, nq),
        in_specs=[
            pl.BlockSpec(memory_space=pltpu.SMEM),
            pl.BlockSpec((None, tq, LANES), lambda b, h, i: (h, b * nq + i, 0)),
            pl.BlockSpec((None, seq, LANES), lambda b, h, i: (h, b, 0)),
            pl.BlockSpec((None, seq, LANES), lambda b, h, i: (h, b, 0)),
            pl.BlockSpec((None, 4, HEAD_DIM), lambda b, h, i: (layer, 0, 0)),
            pl.BlockSpec((None, 1, DIFF_V_DIM), lambda b, h, i: (layer, 0, 0)),
        ],
        out_specs=pl.BlockSpec((tq, LANES), lambda b, h, i: (b * nq + i, h)),
        out_shape=jax.ShapeDtypeStruct((t, DIFF_HEADS * DIFF_V_DIM), BF16),
        scratch_shapes=scratch,
        compiler_params=pltpu.CompilerParams(
            dimension_semantics=("arbitrary", "arbitrary", "arbitrary"),
            vmem_limit_bytes=_vmem_limit(nbytes)),
        name=name,
    )(scal, qd, kd, vd, lp, gsub)


def _mixffn_kernel(x_ref, oa_ref, ob_ref, gate_ref, wb_ref, wo_ref, gn_ref, win_ref, wout_ref,
                   out_ref, h_ref, *, chunk):
    d = x_ref.shape[1]
    up_a = _dot(oa_ref[...], wb_ref[0])
    up_b = _dot(ob_ref[...], wb_ref[1])
    merged = gate_ref[:, :d].astype(F32) * up_a + gate_ref[:, d:].astype(F32) * up_b
    x = x_ref[...] + _dot(merged.astype(BF16), wo_ref[...])
    ms = jnp.mean(x * x, axis=-1, keepdims=True)
    h_ref[...] = (x * lax.rsqrt(ms + EPS) * gn_ref[...]).astype(BF16)
    hidden = wout_ref.shape[0]
    acc = x
    for c in range(hidden // chunk):
        g = _dot(h_ref[...], win_ref[:, c * chunk:(c + 1) * chunk])
        u = _dot(h_ref[...], win_ref[:, hidden + c * chunk:hidden + (c + 1) * chunk])
        a = (g * jax.nn.sigmoid(g) * u).astype(BF16)
        acc = acc + _dot(a, wout_ref[c * chunk:(c + 1) * chunk, :])
    out_ref[...] = acc


def _mixffn(x, oa, ob, gates, wb, wo, gn, win, wout, *, layer, tm, chunk):
    t, d = x.shape
    hidden = wout.shape[1]
    row = lambda i: (i, 0)
    lconst = lambda i: (layer, 0, 0)
    resident = dict(pipeline_mode=pl.Buffered(1))
    kernel = functools.partial(_mixffn_kernel, chunk=chunk)
    weight_bytes = (2 * 512 * d + d * d + 3 * d * hidden) * 2
    nbytes = weight_bytes + 2 * (2 * tm * d * 4 + 2 * tm * 512 * 2 + tm * 2 * d * 2) + tm * d * 10
    return pl.pallas_call(
        kernel,
        grid=(t // tm,),
        in_specs=[
            pl.BlockSpec((tm, d), row),
            pl.BlockSpec((tm, 512), row),
            pl.BlockSpec((tm, 512), row),
            pl.BlockSpec((tm, 2 * d), row),
            pl.BlockSpec((None,) + wb.shape[1:], lambda i: (layer, 0, 0, 0), **resident),
            pl.BlockSpec((None, d, d), lconst, **resident),
            pl.BlockSpec((None, 1, d), lconst),
            pl.BlockSpec((None, d, 2 * hidden), lconst, **resident),
            pl.BlockSpec((None, hidden, d), lconst, **resident),
        ],
        out_specs=pl.BlockSpec((tm, d), row),
        out_shape=jax.ShapeDtypeStruct(x.shape, F32),
        scratch_shapes=[pltpu.VMEM((tm, d), BF16)],
        compiler_params=pltpu.CompilerParams(
            dimension_semantics=("arbitrary",), vmem_limit_bytes=_vmem_limit(nbytes)),
        name="mixffn",
    )(x, oa, ob, gates, wb, wo, gn, win, wout)


def _alibi_slopes():
    return jnp.exp2(-8.0 * jnp.arange(1, N_ALIBI_HEADS + 1, dtype=F32) / N_ALIBI_HEADS)


def _group_mean_matrix():
    idx = jnp.arange(256) // HEAD_DIM
    return jnp.where(idx[:, None] == idx[None, :], 1.0 / HEAD_DIM, 0.0).astype(BF16)


def kernel(x, w_in, b_gate, w_branch, w_o, norm_mix, norm_ffn, qk_norm_swa, qk_norm_diff,
           attn_sinks, diff_lambda, diff_subln, w_ffn_in, w_ffn_out):
    bsz, seq, d = x.shape
    depth = w_in.shape[0]
    t = bsz * seq
    slopes = _alibi_slopes()
    slopes_swa, slopes_diff = slopes[:SWA_Q_HEADS], slopes[SWA_Q_HEADS:]
    bd = _group_mean_matrix()

    w_in_b, w_branch_b, w_o_b = w_in.astype(BF16), w_branch.astype(BF16), w_o.astype(BF16)
    w_ffn_in_b, w_ffn_out_b = w_ffn_in.astype(BF16), w_ffn_out.astype(BF16)
    gmix = norm_mix.astype(F32).reshape(depth, 1, d)
    gffn = norm_ffn.astype(F32).reshape(depth, 1, d)
    bg = b_gate.astype(F32).reshape(depth, 1, 2 * d)
    lp = diff_lambda.astype(F32)
    gsub = diff_subln.astype(F32).reshape(depth, 1, DIFF_V_DIM)
    q_fold = QK_SCALE * LOG2E
    gcol = jnp.concatenate(
        [jnp.tile(qk_norm_swa[:, 0], (1, SWA_Q_HEADS)) * q_fold,
         jnp.tile(qk_norm_swa[:, 1], (1, SWA_KV_HEADS)),
         jnp.ones((depth, SWA_KV_HEADS * HEAD_DIM), F32),
         jnp.tile(qk_norm_diff[:, 0], (1, 2 * DIFF_HEADS)) * q_fold,
         jnp.tile(qk_norm_diff[:, 1], (1, 2 * DIFF_HEADS))], axis=1).astype(F32).reshape(depth, 1, -1)

    def score_bound(qk_norm):
        gains = jnp.max(jnp.abs(qk_norm.astype(F32)), axis=-1)
        return SCORE_BOUND_PER_GAIN * gains[:, 0] * gains[:, 1]

    bound_a, bound_d = score_bound(qk_norm_swa), score_bound(qk_norm_diff)
    sinks2 = attn_sinks.astype(F32) * LOG2E
    ref_a = jnp.maximum(bound_a[:, None] * LOG2E, sinks2)
    scal_a = jnp.concatenate(
        [jnp.broadcast_to(slopes_swa * LOG2E, (depth, SWA_Q_HEADS)), sinks2, ref_a,
         jnp.exp2(sinks2 - ref_a)], axis=1)
    lam_init = jnp.array([0.8 - 0.6 * math.exp(-0.3 * l) for l in range(depth)], F32)
    scal_d = jnp.concatenate(
        [lam_init[:, None], bound_d[:, None] * LOG2E,
         jnp.broadcast_to(slopes_diff * LOG2E, (depth, DIFF_HEADS))], axis=1)

    xf = x.reshape(t, d)
    for l in range(depth):
        qa, ka, va, qd, kd, vd, gates = _proj(xf, gmix, w_in_b, gcol, bd, bg, layer=l, tm=512)

        o_a = lax.cond(
            bound_a[l] <= FIXED_REF_MAX_BOUND,
            lambda a: _swa(*a, bsz=bsz, seq=seq, tq=2048, fixed_reference=True),
            lambda a: _swa(*a, bsz=bsz, seq=seq, tq=512, fixed_reference=False),
            (qa, ka, va, scal_a[l]))

        o_b = lax.cond(
            bound_d[l] <= FIXED_REF_MAX_BOUND,
            lambda a: _diff(*a, layer=l, bsz=bsz, seq=seq, tq=2048, fixed_reference=True),
            lambda a: _diff(*a, layer=l, bsz=bsz, seq=seq, tq=256, fixed_reference=False),
            (qd, kd, vd, scal_d[l], lp, gsub))

        xf = _mixffn(xf, o_a, o_b, gates, w_branch_b, w_o_b, gffn, w_ffn_in_b, w_ffn_out_b,
                     layer=l, tm=512, chunk=256)
    return xf.reshape(bsz, seq, d)
```

```python
import functools
import math

import jax
import jax.numpy as jnp
from jax import lax
from jax.experimental import pallas as pl
from jax.experimental.pallas import tpu as pltpu

F32 = jnp.float32
BF16 = jnp.bfloat16

HEAD_DIM = 64
BLOCK = 128
SWA_Q_HEADS = 8
SWA_KV_HEADS = 2
SWA_GROUP = SWA_Q_HEADS // SWA_KV_HEADS
DIFF_HEADS = 4
DIFF_V_DIM = 2 * HEAD_DIM
N_ALIBI_HEADS = SWA_Q_HEADS + DIFF_HEADS
NEG = -1e30
EPS = 1e-6
QK_SCALE = HEAD_DIM ** -0.5
LOG2E = math.log2(math.e)
SCORE_BOUND_PER_GAIN = HEAD_DIM * QK_SCALE
FIXED_REF_MAX_BOUND = 20.0

V7X_VMEM_BYTES = 64 * 1024 * 1024
LANES = 128


def _vmem_limit(nbytes):
    return int(min(max(2 * nbytes, 32 * 1024 * 1024), V7X_VMEM_BYTES - 6 * 1024 * 1024))


def _dot(a, b):
    return jnp.dot(a, b, preferred_element_type=F32)


def _dot_nt(a, b):
    return lax.dot_general(a, b, (((1,), (1,)), ((), ())), preferred_element_type=F32)


PROJ_CHUNK = 256
PROJ_NORMED = 1792


def _proj_kernel(x_ref, gmix_ref, w_ref, gcol_ref, bd_ref, bg_ref,
                 qa_ref, ka_ref, va_ref, qd_ref, kd_ref, vd_ref, gate_ref, h_ref, y_ref):
    tm = x_ref.shape[0]
    x = x_ref[...]
    ms = jnp.mean(x * x, axis=-1, keepdims=True)
    h_ref[...] = (x * lax.rsqrt(ms + EPS) * gmix_ref[...]).astype(BF16)

    def proj(c0, width):
        return _dot(h_ref[...], w_ref[:, c0:c0 + width])

    nchunk = PROJ_NORMED // PROJ_CHUNK
    y_ref[...] = proj(0, PROJ_NORMED)
    sq = jnp.concatenate(
        [jnp.square(y_ref[:, PROJ_CHUNK * c:PROJ_CHUNK * (c + 1)]).astype(BF16)
         for c in range(nchunk)], axis=0)
    inv = lax.rsqrt(_dot(sq, bd_ref[...]) + EPS)

    def normed(c):
        cs = slice(PROJ_CHUNK * c, PROJ_CHUNK * (c + 1))
        return (y_ref[:, cs] * inv[c * tm:(c + 1) * tm] * gcol_ref[:, cs]).astype(BF16)

    for c in range(2):
        qa_ref[:, PROJ_CHUNK * c:PROJ_CHUNK * (c + 1)] = normed(c)
    ka_ref[...] = normed(2)[:, :LANES]
    va_ref[...] = y_ref[:, 2 * PROJ_CHUNK + LANES:3 * PROJ_CHUNK].astype(BF16)
    for c in range(2):
        y = normed(3 + c)
        qd_ref[2 * c] = y[:, :LANES]
        qd_ref[2 * c + 1] = y[:, LANES:]
    for c in range(2):
        y = normed(5 + c)
        kd_ref[2 * c] = y[:, :LANES]
        kd_ref[2 * c + 1] = y[:, LANES:]
    y = proj(PROJ_NORMED, DIFF_HEADS * LANES).astype(BF16)
    for hd in range(DIFF_HEADS):
        vd_ref[hd] = y[:, hd * LANES:(hd + 1) * LANES]
    g0 = PROJ_NORMED + DIFF_HEADS * LANES
    gate_ref[...] = jax.nn.sigmoid(proj(g0, gate_ref.shape[1]) + bg_ref[...]).astype(BF16)


def _proj(x, gmix, w, gcol, bd, bg, *, layer, tm):
    t, d = x.shape
    ncols = w.shape[2]
    grid = (t // tm,)
    const = lambda i: (0, 0)
    out_shape = (
        jax.ShapeDtypeStruct((t, 512), BF16),
        jax.ShapeDtypeStruct((t, 128), BF16),
        jax.ShapeDtypeStruct((t, 128), BF16),
        jax.ShapeDtypeStruct((DIFF_HEADS, t, 128), BF16),
        jax.ShapeDtypeStruct((DIFF_HEADS, t, 128), BF16),
        jax.ShapeDtypeStruct((DIFF_HEADS, t, 128), BF16),
        jax.ShapeDtypeStruct((t, 2 * d), BF16),
    )
    row = lambda i: (i, 0)
    hrow = lambda i: (0, i, 0)
    out_specs = (
        pl.BlockSpec((tm, 512), row),
        pl.BlockSpec((tm, 128), row),
        pl.BlockSpec((tm, 128), row),
        pl.BlockSpec((DIFF_HEADS, tm, 128), hrow),
        pl.BlockSpec((DIFF_HEADS, tm, 128), hrow),
        pl.BlockSpec((DIFF_HEADS, tm, 128), hrow),
        pl.BlockSpec((tm, 2 * d), row),
    )
    lconst = lambda i: (layer, 0, 0)
    in_specs = [
        pl.BlockSpec((tm, d), row),
        pl.BlockSpec((None, 1, d), lconst),
        pl.BlockSpec((None, d, ncols), lconst, pipeline_mode=pl.Buffered(1)),
        pl.BlockSpec((None, 1, gcol.shape[2]), lconst),
        pl.BlockSpec(bd.shape, const),
        pl.BlockSpec((None, 1, 2 * d), lconst),
    ]
    nbytes = (2 * tm * d * 4 + d * ncols * 2 + 2 * tm * ncols * 2 + tm * d * 2
              + 3 * tm * PROJ_NORMED * 4)
    return pl.pallas_call(
        _proj_kernel,
        grid=grid,
        in_specs=in_specs,
        out_specs=out_specs,
        out_shape=out_shape,
        scratch_shapes=[pltpu.VMEM((tm, d), BF16), pltpu.VMEM((tm, PROJ_NORMED), F32)],
        compiler_params=pltpu.CompilerParams(
            dimension_semantics=("arbitrary",), vmem_limit_bytes=_vmem_limit(nbytes)),
        name="proj",
    )(x, gmix, w, gcol, bd, bg)


SWA_FILL = 512
MASKED_LOGIT_SHIFT = 1e30


def _swa_fixed_kernel(scal_ref, q_ref, k_ref, v_ref, o_ref, kbuf_ref, vbuf_ref, tab_ref,
                      *, tq, seq):
    b = pl.program_id(0)
    i = pl.program_id(1)
    nblk = tq // BLOCK
    rows = SWA_GROUP * BLOCK
    win = 2 * BLOCK

    @pl.when(jnp.logical_and(b == 0, i == 0))
    def _build_tables():
        row = lax.broadcasted_iota(jnp.int32, (rows, win), 0)
        col = lax.broadcasted_iota(jnp.int32, (rows, win), 1)
        head = lax.shift_right_logical(row, 7)
        dist = jnp.bitwise_and(row, BLOCK - 1) + BLOCK - col
        valid = jnp.logical_and(dist >= 0, dist < BLOCK)
        has_key = jnp.logical_and(valid, col >= BLOCK)
        distf = dist.astype(F32)
        for g in range(SWA_KV_HEADS):
            def per_head(base):
                h0 = SWA_GROUP * g
                return jnp.where(head == 0, scal_ref[base + h0],
                                 jnp.where(head == 1, scal_ref[base + h0 + 1],
                                           jnp.where(head == 2, scal_ref[base + h0 + 2],
                                                     scal_ref[base + h0 + 3])))
            t = per_head(0) * distf + per_head(2 * SWA_Q_HEADS)
            tab_ref[g] = jnp.where(valid, t, MASKED_LOGIT_SHIFT)
            tab_ref[SWA_KV_HEADS + g] = jnp.where(has_key, t, MASKED_LOGIT_SHIFT)

    @pl.when(i == 0)
    def _stage_kv():
        kbuf_ref[:, 0:BLOCK, :] = jnp.zeros((SWA_KV_HEADS, BLOCK, LANES), BF16)
        vbuf_ref[:, 0:BLOCK, :] = jnp.zeros((SWA_KV_HEADS, BLOCK, 2 * LANES), BF16)
        lane = lax.broadcasted_iota(jnp.int32, (SWA_FILL, LANES), 1)
        first = lane < HEAD_DIM

        def body(c, carry):
            r0 = pl.multiple_of(c * SWA_FILL, SWA_FILL)
            dst = pl.ds(pl.multiple_of(r0 + BLOCK, BLOCK), SWA_FILL)
            for src_ref, dst_ref in ((k_ref, kbuf_ref), (v_ref, vbuf_ref)):
                x = src_ref[pl.ds(r0, SWA_FILL), :].astype(F32)
                xr = pltpu.roll(x, HEAD_DIM, 1)
                dst_ref[0, dst, 0:LANES] = jnp.where(first, x, xr).astype(BF16)
                dst_ref[1, dst, 0:LANES] = jnp.where(first, xr, x).astype(BF16)
            vbuf_ref[:, dst, LANES:2 * LANES] = jnp.ones((SWA_KV_HEADS, SWA_FILL, LANES), BF16)
            return carry
        lax.fori_loop(0, seq // SWA_FILL, body, 0)

    lane = lax.broadcasted_iota(jnp.int32, (BLOCK, LANES), 1)
    first = lane < HEAD_DIM
    for r in range(nblk):
        n = i * nblk + r
        w0 = pl.multiple_of(n * BLOCK, BLOCK)
        tsel = jnp.where(n == 0, SWA_KV_HEADS, 0)
        rs = slice(r * BLOCK, (r + 1) * BLOCK)
        for g in range(SWA_KV_HEADS):
            slabs = [q_ref[rs, (2 * g + a) * LANES:(2 * g + a + 1) * LANES] for a in range(2)]
            zero = jnp.zeros_like(slabs[0])
            lhs = jnp.concatenate(
                [jnp.where(first, slabs[0], zero), jnp.where(first, zero, slabs[0]),
                 jnp.where(first, slabs[1], zero), jnp.where(first, zero, slabs[1])], axis=0)
            s = _dot_nt(lhs, kbuf_ref[g, pl.ds(w0, win), :])
            p = jnp.exp2(s - tab_ref[tsel + g]).astype(BF16)
            out = _dot(p, vbuf_ref[g, pl.ds(w0, win), :])
            for a in range(2):
                halves = []
                for k in range(2):
                    hq = SWA_GROUP * g + 2 * a + k
                    blk = out[(2 * a + k) * BLOCK:(2 * a + k + 1) * BLOCK]
                    denom = blk[:, LANES:2 * LANES] + scal_ref[3 * SWA_Q_HEADS + hq]
                    halves.append(blk[:, 0:LANES] / denom)
                o_ref[rs, (2 * g + a) * LANES:(2 * g + a + 1) * LANES] = (
                    jnp.where(first, halves[0], halves[1]).astype(BF16))


def _swa_online_kernel(scal_ref, q_ref, k_ref, v_ref, o_ref, *, tq):
    i = pl.program_id(1)
    qi = lax.broadcasted_iota(jnp.int32, (BLOCK, BLOCK), 0)
    kj = lax.broadcasted_iota(jnp.int32, (BLOCK, BLOCK), 1)
    lower = kj <= qi
    dist = jnp.where(lower, qi - kj, BLOCK + qi - kj).astype(F32)
    for r in range(tq // BLOCK):
        n = i * (tq // BLOCK) + r
        cur = pl.multiple_of(n * BLOCK, BLOCK)
        prev = pl.multiple_of(jnp.maximum(n - 1, 0) * BLOCK, BLOCK)
        valid = jnp.logical_or(lower, n > 0)
        k_cur = k_ref[pl.ds(cur, BLOCK), :]
        k_prev = k_ref[pl.ds(prev, BLOCK), :]
        v_cur = v_ref[pl.ds(cur, BLOCK), :]
        v_prev = v_ref[pl.ds(prev, BLOCK), :]
        outs = []
        for hq in range(SWA_Q_HEADS):
            g = hq // SWA_GROUP
            gs = slice(g * HEAD_DIM, (g + 1) * HEAD_DIM)
            qh = q_ref[r * BLOCK:(r + 1) * BLOCK, hq * HEAD_DIM:(hq + 1) * HEAD_DIM]
            s_cur = _dot_nt(qh, k_cur[:, gs])
            s_prev = _dot_nt(qh, k_prev[:, gs])
            s = jnp.where(lower, s_cur, s_prev) - scal_ref[hq] * dist
            s = jnp.where(valid, s, NEG)
            sink = scal_ref[SWA_Q_HEADS + hq]
            m = jnp.maximum(jnp.max(s, axis=-1, keepdims=True), sink)
            e = jnp.exp2(s - m)
            denom = jnp.sum(e, axis=-1, keepdims=True) + jnp.exp2(sink - m)
            p = e / denom
            p_cur = jnp.where(lower, p, 0.0).astype(BF16)
            p_prev = jnp.where(lower, 0.0, p).astype(BF16)
            outs.append(_dot(p_cur, v_cur[:, gs]) + _dot(p_prev, v_prev[:, gs]))
        o_ref[r * BLOCK:(r + 1) * BLOCK, :] = jnp.concatenate(outs, axis=-1).astype(BF16)


def _swa(qa, ka, va, scal, *, bsz, seq, tq, fixed_reference):
    nq = seq // tq
    nbytes = 2 * (2 * tq * 512 * 2 + 2 * seq * LANES * 2)
    if fixed_reference:
        kernel = functools.partial(_swa_fixed_kernel, tq=tq, seq=seq)
        rows = SWA_GROUP * BLOCK
        scratch = [
            pltpu.VMEM((SWA_KV_HEADS, seq + BLOCK, LANES), BF16),
            pltpu.VMEM((SWA_KV_HEADS, seq + BLOCK, 2 * LANES), BF16),
            pltpu.VMEM((2 * SWA_KV_HEADS, rows, 2 * BLOCK), F32),
        ]
        nbytes += (SWA_KV_HEADS * (seq + BLOCK) * 3 * LANES * 2 + 2 * SWA_KV_HEADS * rows * 2 * BLOCK * 4
                   + 8 * rows * 2 * BLOCK * 4)
        name = "swa"
    else:
        kernel = functools.partial(_swa_online_kernel, tq=tq)
        scratch = []
        name = "swa_online"
    return pl.pallas_call(
        kernel,
        grid=(bsz, nq),
        in_specs=[
            pl.BlockSpec(memory_space=pltpu.SMEM),
            pl.BlockSpec((tq, 512), lambda b, i: (b * nq + i, 0)),
            pl.BlockSpec((seq, LANES), lambda b, i: (b, 0)),
            pl.BlockSpec((seq, LANES), lambda b, i: (b, 0)),
        ],
        out_specs=pl.BlockSpec((tq, 512), lambda b, i: (b * nq + i, 0)),
        out_shape=jax.ShapeDtypeStruct(qa.shape, BF16),
        scratch_shapes=scratch,
        compiler_params=pltpu.CompilerParams(
            dimension_semantics=("arbitrary", "arbitrary"), vmem_limit_bytes=_vmem_limit(nbytes)),
        name=name,
    )(scal, qa, ka, va)


def _stack_q(q, tq):
    lane = lax.broadcasted_iota(jnp.int32, (tq, LANES), 1)
    zero = jnp.zeros_like(q)
    return jnp.where(lane < HEAD_DIM, q, zero), jnp.where(lane < HEAD_DIM, zero, q)


def _diff_finish(o1, o2, lam_init, lp_ref, g_ref):
    lp = lp_ref[...]
    lam = (jnp.exp(jnp.sum(lp[0:1] * lp[1:2], axis=-1, keepdims=True))
           - jnp.exp(jnp.sum(lp[2:3] * lp[3:4], axis=-1, keepdims=True)) + lam_init)
    o = o1 - lam * o2
    ms = jnp.mean(o * o, axis=-1, keepdims=True)
    return (o * lax.rsqrt(ms + EPS) * g_ref[...] * (1.0 - lam_init)).astype(BF16)


def _causal_keep(i, tq, start, width):
    col = start + lax.broadcasted_iota(jnp.int32, (1, width), 1)
    row = i * tq + lax.rem(lax.broadcasted_iota(jnp.int32, (2 * tq, 1), 0), tq)
    return col <= row


AUG_K_BASE = 3
AUG_Q_BASE = 9
DIFF_PART_SHIFT = 9
DIFF_PART = 1 << DIFF_PART_SHIFT
DIFF_KEY_BLOCK = 1024


def _split3(val):
    hi = val.astype(BF16).astype(F32)
    r1 = val - hi
    mid = r1.astype(BF16).astype(F32)
    lo = (r1 - mid).astype(BF16).astype(F32)
    return hi, mid, lo


def _aug_templates(slope2, rows, tmpl_ref):
    r = lax.broadcasted_iota(jnp.int32, (rows, LANES), 0).astype(F32)
    hi, mid, lo = _split3(slope2 * r)
    lane = lax.broadcasted_iota(jnp.int32, (rows, LANES), 1)
    parts = jnp.where(jnp.logical_or(lane == 0, lane == 6), hi,
                      jnp.where(jnp.logical_or(lane == 1, lane == 7), mid, lo))
    k_ones = jnp.logical_and(lane >= 6, lane < 12)
    tmpl_ref[0] = jnp.where(lane < 3, parts, jnp.where(k_ones, 1.0, 0.0))
    tmpl_ref[1] = jnp.where(lane < 6, 1.0, jnp.where(lane < 9, -parts, 0.0))


def _with_scalar_lanes(template, value, base):
    hi, mid, lo = _split3(jnp.full((1, LANES), value, F32))
    lane = lax.broadcasted_iota(jnp.int32, (1, LANES), 1)
    terms = jnp.where(lane == base, hi, jnp.where(lane == base + 1, mid, lo))
    here = jnp.logical_and(lane >= base, lane < base + 3)
    return jnp.where(here, terms, template).astype(BF16)


def _diff_fixed_kernel(scal_ref, q_ref, k_ref, v_ref, lp_ref, g_ref, o_ref,
                       kaug_ref, vaug_ref, qs_ref, acc_ref, tmpl_ref, *, tq, seq):
    h = pl.program_id(1)
    i = pl.program_id(2)
    lam_init = scal_ref[0]
    bound2 = scal_ref[1]
    slope2 = scal_ref[2 + h]
    part = DIFF_PART
    nparts = tq // part

    @pl.when(i == 0)
    def _build_kv():
        _aug_templates(slope2, part, tmpl_ref)

        def body(c, carry):
            r0 = pl.multiple_of(c * part, part)
            kaug_ref[pl.ds(r0, part), 0:LANES] = k_ref[pl.ds(r0, part), :]
            kaug_ref[pl.ds(r0, part), LANES:2 * LANES] = _with_scalar_lanes(
                tmpl_ref[0], slope2 * r0.astype(F32), AUG_K_BASE)
            vaug_ref[pl.ds(r0, part), 0:LANES] = v_ref[pl.ds(r0, part), :]
            vaug_ref[pl.ds(r0, part), LANES:2 * LANES] = jnp.ones((part, LANES), BF16)
            return carry
        lax.fori_loop(0, seq // part, body, 0)

    for c in range(nparts):
        q1, q2 = _stack_q(q_ref[c * part:(c + 1) * part, :], part)
        row0 = (i * tq + c * part).astype(F32)
        aug = _with_scalar_lanes(tmpl_ref[1], -(slope2 * row0 + bound2), AUG_Q_BASE)
        base = 2 * part * c
        qs_ref[base:base + part, 0:LANES] = q1
        qs_ref[base + part:base + 2 * part, 0:LANES] = q2
        qs_ref[base:base + part, LANES:2 * LANES] = aug
        qs_ref[base + part:base + 2 * part, LANES:2 * LANES] = aug
    acc_ref[...] = jnp.zeros(acc_ref.shape, F32)

    def full_block(j):
        start = pl.multiple_of(j * DIFF_KEY_BLOCK, DIFF_KEY_BLOCK)
        p = jnp.exp2(_dot_nt(qs_ref[...], kaug_ref[pl.ds(start, DIFF_KEY_BLOCK), :]))
        acc_ref[...] += _dot(p.astype(BF16), vaug_ref[pl.ds(start, DIFF_KEY_BLOCK), :])

    def block_pair(j, carry):
        full_block(2 * j)
        full_block(2 * j + 1)
        return carry

    lax.fori_loop(0, i * (tq // (2 * DIFF_KEY_BLOCK)), block_pair, 0)

    col = lax.broadcasted_iota(jnp.int32, (1, part), 1)
    for c in range(nparts):
        start = pl.multiple_of(i * tq + c * part, part)
        lo = 2 * part * c
        srow = lax.broadcasted_iota(jnp.int32, (2 * tq - lo, 1), 0)
        row = (c + lax.shift_right_logical(srow, DIFF_PART_SHIFT + 1)) * part + jnp.bitwise_and(srow, part - 1)
        p = jnp.exp2(_dot_nt(qs_ref[lo:2 * tq, :], kaug_ref[pl.ds(start, part), :]))
        p = jnp.where(col + c * part <= row, p, 0.0).astype(BF16)
        acc_ref[lo:2 * tq, :] += _dot(p, vaug_ref[pl.ds(start, part), :])

    for c in range(nparts):
        base = 2 * part * c
        a1 = acc_ref[base:base + part, :]
        a2 = acc_ref[base + part:base + 2 * part, :]
        o_ref[c * part:(c + 1) * part, :] = _diff_finish(
            a1[:, 0:LANES] / a1[:, LANES:2 * LANES], a2[:, 0:LANES] / a2[:, LANES:2 * LANES],
            lam_init, lp_ref, g_ref)


def _diff_online_kernel(scal_ref, q_ref, k_ref, v_ref, lp_ref, g_ref, o_ref,
                        qs_ref, m_ref, l_ref, acc_ref, *, tq):
    h = pl.program_id(1)
    i = pl.program_id(2)
    lam_init = scal_ref[0]
    slope2 = scal_ref[2 + h]

    q1, q2 = _stack_q(q_ref[...], tq)
    qs_ref[0:tq, :] = q1
    qs_ref[tq:2 * tq, :] = q2
    m_ref[...] = jnp.full(m_ref.shape, NEG, F32)
    l_ref[...] = jnp.zeros(l_ref.shape, F32)
    acc_ref[...] = jnp.zeros(acc_ref.shape, F32)

    def step(j, masked):
        start = pl.multiple_of(j * tq, tq)
        s = _dot_nt(qs_ref[...], k_ref[pl.ds(start, tq), :])
        col = start + lax.broadcasted_iota(jnp.int32, (1, tq), 1)
        s = s + slope2 * col.astype(F32)
        if masked:
            s = jnp.where(_causal_keep(i, tq, start, tq), s, NEG)
        m_prev = m_ref[...]
        m_new = jnp.maximum(m_prev, jnp.max(s, axis=-1, keepdims=True))
        alpha = jnp.exp2(m_prev - m_new)
        p = jnp.exp2(s - m_new)
        l_ref[...] = alpha * l_ref[...] + jnp.sum(p, axis=-1, keepdims=True)
        acc_ref[...] = alpha * acc_ref[...] + _dot(p.astype(BF16), v_ref[pl.ds(start, tq), :])
        m_ref[...] = m_new

    def body(j, carry):
        step(j, False)
        return carry

    lax.fori_loop(0, i, body, 0)
    step(i, True)

    acc = acc_ref[...]
    l = l_ref[...]
    o_ref[...] = _diff_finish(acc[0:tq] / l[0:tq], acc[tq:2 * tq] / l[tq:2 * tq],
                              lam_init, lp_ref, g_ref)


def _diff(qd, kd, vd, scal, lp, gsub, *, layer, bsz, seq, tq, fixed_reference):
    nq = seq // tq
    t = bsz * seq
    if fixed_reference:
        assert tq % (2 * DIFF_KEY_BLOCK) == 0 and seq % tq == 0
        kernel = functools.partial(_diff_fixed_kernel, tq=tq, seq=seq)
        scratch = [
            pltpu.VMEM((seq, 2 * LANES), BF16),
            pltpu.VMEM((seq, 2 * LANES), BF16),
            pltpu.VMEM((2 * tq, 2 * LANES), BF16),
            pltpu.VMEM((2 * tq, 2 * LANES), F32),
            pltpu.VMEM((2, DIFF_PART, LANES), F32),
        ]
        scratch_bytes = (2 * seq * 2 * LANES * 2 + 2 * tq * 2 * LANES * 6 + 2 * DIFF_PART * LANES * 4
                         + 2 * tq * DIFF_KEY_BLOCK * 6)
        name = "diff"
    else:
        kernel = functools.partial(_diff_online_kernel, tq=tq)
        scratch = [
            pltpu.VMEM((2 * tq, LANES), BF16),
            pltpu.VMEM((2 * tq, 1), F32),
            pltpu.VMEM((2 * tq, 1), F32),
            pltpu.VMEM((2 * tq, LANES), F32),
        ]
        scratch_bytes = 2 * tq * (LANES * 6 + 8) + 6 * 2 * tq * tq * 4
        name = "diff_online"
    nbytes = 2 * (2 * seq * LANES * 2 + 2 * tq * LANES * 2) + scratch_bytes
    return pl.pallas_call(
        kernel,
        grid=(bsz, DIFF_HEADS, nq),
        in_specs=[
            pl.BlockSpec(memory_space=pltpu.SMEM),
            pl.BlockSpec((None, tq, LANES), lambda b, h, i: (h, b * nq + i, 0)),
            pl.BlockSpec((None, seq, LANES), lambda b, h, i: (h, b, 0)),
            pl.BlockSpec((None, seq, LANES), lambda b, h, i: (h, b, 0)),
            pl.BlockSpec((None, 4, HEAD_DIM), lambda b, h, i: (layer, 0, 0)),
            pl.BlockSpec((None, 1, DIFF_V_DIM), lambda b, h, i: (layer, 0, 0)),
        ],
        out_specs=pl.BlockSpec((tq, LANES), lambda b, h, i: (b * nq + i, h)),
        out_shape=jax.ShapeDtypeStruct((t, DIFF_HEADS * DIFF_V_DIM), BF16),
        scratch_shapes=scratch,
        compiler_params=pltpu.CompilerParams(
            dimension_semantics=("arbitrary", "arbitrary", "arbitrary"),
            vmem_limit_bytes=_vmem_limit(nbytes)),
        name=name,
    )(scal, qd, kd, vd, lp, gsub)


def _mixffn_kernel(x_ref, oa_ref, ob_ref, gate_ref, wb_ref, wo_ref, gn_ref, win_ref, wout_ref,
                   out_ref, h_ref, *, chunk):
    d = x_ref.shape[1]
    up_a = _dot(oa_ref[...], wb_ref[0])
    up_b = _dot(ob_ref[...], wb_ref[1])
    merged = gate_ref[:, :d].astype(F32) * up_a + gate_ref[:, d:].astype(F32) * up_b
    x = x_ref[...] + _dot(merged.astype(BF16), wo_ref[...])
    ms = jnp.mean(x * x, axis=-1, keepdims=True)
    h_ref[...] = (x * lax.rsqrt(ms + EPS) * gn_ref[...]).astype(BF16)
    hidden = wout_ref.shape[0]
    acc = x
    for c in range(hidden // chunk):
        g = _dot(h_ref[...], win_ref[:, c * chunk:(c + 1) * chunk])
        u = _dot(h_ref[...], win_ref[:, hidden + c * chunk:hidden + (c + 1) * chunk])
        a = (g * jax.nn.sigmoid(g) * u).astype(BF16)
        acc = acc + _dot(a, wout_ref[c * chunk:(c + 1) * chunk, :])
    out_ref[...] = acc


def _mixffn(x, oa, ob, gates, wb, wo, gn, win, wout, *, layer, tm, chunk):
    t, d = x.shape
    hidden = wout.shape[1]
    row = lambda i: (i, 0)
    lconst = lambda i: (layer, 0, 0)
    resident = dict(pipeline_mode=pl.Buffered(1))
    kernel = functools.partial(_mixffn_kernel, chunk=chunk)
    weight_bytes = (2 * 512 * d + d * d + 3 * d * hidden) * 2
    nbytes = weight_bytes + 2 * (2 * tm * d * 4 + 2 * tm * 512 * 2 + tm * 2 * d * 2) + tm * d * 10
    return pl.pallas_call(
        kernel,
        grid=(t // tm,),
        in_specs=[
            pl.BlockSpec((tm, d), row),
            pl.BlockSpec((tm, 512), row),
            pl.BlockSpec((tm, 512), row),
            pl.BlockSpec((tm, 2 * d), row),
            pl.BlockSpec((None,) + wb.shape[1:], lambda i: (layer, 0, 0, 0), **resident),
            pl.BlockSpec((None, d, d), lconst, **resident),
            pl.BlockSpec((None, 1, d), lconst),
            pl.BlockSpec((None, d, 2 * hidden), lconst, **resident),
            pl.BlockSpec((None, hidden, d), lconst, **resident),
        ],
        out_specs=pl.BlockSpec((tm, d), row),
        out_shape=jax.ShapeDtypeStruct(x.shape, F32),
        scratch_shapes=[pltpu.VMEM((tm, d), BF16)],
        compiler_params=pltpu.CompilerParams(
            dimension_semantics=("arbitrary",), vmem_limit_bytes=_vmem_limit(nbytes)),
        name="mixffn",
    )(x, oa, ob, gates, wb, wo, gn, win, wout)


def _alibi_slopes():
    return jnp.exp2(-8.0 * jnp.arange(1, N_ALIBI_HEADS + 1, dtype=F32) / N_ALIBI_HEADS)


def _group_mean_matrix():
    idx = jnp.arange(256) // HEAD_DIM
    return jnp.where(idx[:, None] == idx[None, :], 1.0 / HEAD_DIM, 0.0).astype(BF16)


def kernel(x, w_in, b_gate, w_branch, w_o, norm_mix, norm_ffn, qk_norm_swa, qk_norm_diff,
           attn_sinks, diff_lambda, diff_subln, w_ffn_in, w_ffn_out):
    bsz, seq, d = x.shape
    depth = w_in.shape[0]
    t = bsz * seq
    slopes = _alibi_slopes()
    slopes_swa, slopes_diff = slopes[:SWA_Q_HEADS], slopes[SWA_Q_HEADS:]
    bd = _group_mean_matrix()

    w_in_b, w_branch_b, w_o_b = w_in.astype(BF16), w_branch.astype(BF16), w_o.astype(BF16)
    w_ffn_in_b, w_ffn_out_b = w_ffn_in.astype(BF16), w_ffn_out.astype(BF16)
    gmix = norm_mix.astype(F32).reshape(depth, 1, d)
    gffn = norm_ffn.astype(F32).reshape(depth, 1, d)
    bg = b_gate.astype(F32).reshape(depth, 1, 2 * d)
    lp = diff_lambda.astype(F32)
    gsub = diff_subln.astype(F32).reshape(depth, 1, DIFF_V_DIM)
    q_fold = QK_SCALE * LOG2E
    gcol = jnp.concatenate(
        [jnp.tile(qk_norm_swa[:, 0], (1, SWA_Q_HEADS)) * q_fold,
         jnp.tile(qk_norm_swa[:, 1], (1, SWA_KV_HEADS)),
         jnp.ones((depth, SWA_KV_HEADS * HEAD_DIM), F32),
         jnp.tile(qk_norm_diff[:, 0], (1, 2 * DIFF_HEADS)) * q_fold,
         jnp.tile(qk_norm_diff[:, 1], (1, 2 * DIFF_HEADS))], axis=1).astype(F32).reshape(depth, 1, -1)

    def score_bound(qk_norm):
        gains = jnp.max(jnp.abs(qk_norm.astype(F32)), axis=-1)
        return SCORE_BOUND_PER_GAIN * gains[:, 0] * gains[:, 1]

    bound_a, bound_d = score_bound(qk_norm_swa), score_bound(qk_norm_diff)
    sinks2 = attn_sinks.astype(F32) * LOG2E
    ref_a = jnp.maximum(bound_a[:, None] * LOG2E, sinks2)
    scal_a = jnp.concatenate(
        [jnp.broadcast_to(slopes_swa * LOG2E, (depth, SWA_Q_HEADS)), sinks2, ref_a,
         jnp.exp2(sinks2 - ref_a)], axis=1)
    lam_init = jnp.array([0.8 - 0.6 * math.exp(-0.3 * l) for l in range(depth)], F32)
    scal_d = jnp.concatenate(
        [lam_init[:, None], bound_d[:, None] * LOG2E,
         jnp.broadcast_to(slopes_diff * LOG2E, (depth, DIFF_HEADS))], axis=1)

    xf = x.reshape(t, d)
    for l in range(depth):
        qa, ka, va, qd, kd, vd, gates = _proj(xf, gmix, w_in_b, gcol, bd, bg, layer=l, tm=1024)

        o_a = lax.cond(
            bound_a[l] <= FIXED_REF_MAX_BOUND,
            lambda a: _swa(*a, bsz=bsz, seq=seq, tq=2048, fixed_reference=True),
            lambda a: _swa(*a, bsz=bsz, seq=seq, tq=512, fixed_reference=False),
            (qa, ka, va, scal_a[l]))

        o_b = lax.cond(
            bound_d[l] <= FIXED_REF_MAX_BOUND,
            lambda a: _diff(*a, layer=l, bsz=bsz, seq=seq, tq=4096, fixed_reference=True),
            lambda a: _diff(*a, layer=l, bsz=bsz, seq=seq, tq=256, fixed_reference=False),
            (qd, kd, vd, scal_d[l], lp, gsub))

        xf = _mixffn(xf, o_a, o_b, gates, w_branch_b, w_o_b, gffn, w_ffn_in_b, w_ffn_out_b,
                     layer=l, tm=512, chunk=256)
    return xf.reshape(bsz, seq, d)
```

```python
import functools
import math

import jax
import jax.numpy as jnp
from jax import lax
from jax.experimental import pallas as pl
from jax.experimental.pallas import tpu as pltpu

F32 = jnp.float32
BF16 = jnp.bfloat16

HEAD_DIM = 64
BLOCK = 128
SWA_Q_HEADS = 8
SWA_KV_HEADS = 2
SWA_GROUP = SWA_Q_HEADS // SWA_KV_HEADS
DIFF_HEADS = 4
DIFF_V_DIM = 2 * HEAD_DIM
N_ALIBI_HEADS = SWA_Q_HEADS + DIFF_HEADS
NEG = -1e30
EPS = 1e-6
QK_SCALE = HEAD_DIM ** -0.5
LOG2E = math.log2(math.e)
SCORE_BOUND_PER_GAIN = HEAD_DIM * QK_SCALE
FIXED_REF_MAX_BOUND = 20.0

V7X_VMEM_BYTES = 64 * 1024 * 1024
LANES = 128


def _vmem_limit(nbytes):
    return int(min(max(2 * nbytes, 32 * 1024 * 1024), V7X_VMEM_BYTES - 8 * 1024 * 1024))


def _dot(a, b):
    return jnp.dot(a, b, preferred_element_type=F32)


def _dot_nt(a, b):
    return lax.dot_general(a, b, (((1,), (1,)), ((), ())), preferred_element_type=F32)


PROJ_CHUNK = 256
PROJ_NORMED = 1792


def _proj_kernel(x_ref, gmix_ref, w_ref, gcol_ref, bd_ref, bg_ref,
                 qa_ref, ka_ref, va_ref, qd_ref, kd_ref, vd_ref, gate_ref, h_ref, y_ref):
    tm = x_ref.shape[0]
    x = x_ref[...]
    ms = jnp.mean(x * x, axis=-1, keepdims=True)
    h_ref[...] = (x * lax.rsqrt(ms + EPS) * gmix_ref[...]).astype(BF16)

    def proj(c0, width):
        return _dot(h_ref[...], w_ref[:, c0:c0 + width])

    nchunk = PROJ_NORMED // PROJ_CHUNK
    y_ref[...] = proj(0, PROJ_NORMED)
    sq = jnp.concatenate(
        [jnp.square(y_ref[:, PROJ_CHUNK * c:PROJ_CHUNK * (c + 1)]).astype(BF16)
         for c in range(nchunk)], axis=0)
    inv = lax.rsqrt(_dot(sq, bd_ref[...]) + EPS)

    def normed(c):
        cs = slice(PROJ_CHUNK * c, PROJ_CHUNK * (c + 1))
        return (y_ref[:, cs] * inv[c * tm:(c + 1) * tm] * gcol_ref[:, cs]).astype(BF16)

    for c in range(2):
        qa_ref[:, PROJ_CHUNK * c:PROJ_CHUNK * (c + 1)] = normed(c)
    ka_ref[...] = normed(2)[:, :LANES]
    va_ref[...] = y_ref[:, 2 * PROJ_CHUNK + LANES:3 * PROJ_CHUNK].astype(BF16)
    for c in range(2):
        y = normed(3 + c)
        qd_ref[2 * c] = y[:, :LANES]
        qd_ref[2 * c + 1] = y[:, LANES:]
    for c in range(2):
        y = normed(5 + c)
        kd_ref[2 * c] = y[:, :LANES]
        kd_ref[2 * c + 1] = y[:, LANES:]
    y = proj(PROJ_NORMED, DIFF_HEADS * LANES).astype(BF16)
    for hd in range(DIFF_HEADS):
        vd_ref[hd] = y[:, hd * LANES:(hd + 1) * LANES]
    g0 = PROJ_NORMED + DIFF_HEADS * LANES
    gate_ref[...] = jax.nn.sigmoid(proj(g0, gate_ref.shape[1]) + bg_ref[...]).astype(BF16)


def _proj(x, gmix, w, gcol, bd, bg, *, layer, tm):
    t, d = x.shape
    ncols = w.shape[2]
    grid = (t // tm,)
    const = lambda i: (0, 0)
    out_shape = (
        jax.ShapeDtypeStruct((t, 512), BF16),
        jax.ShapeDtypeStruct((t, 128), BF16),
        jax.ShapeDtypeStruct((t, 128), BF16),
        jax.ShapeDtypeStruct((DIFF_HEADS, t, 128), BF16),
        jax.ShapeDtypeStruct((DIFF_HEADS, t, 128), BF16),
        jax.ShapeDtypeStruct((DIFF_HEADS, t, 128), BF16),
        jax.ShapeDtypeStruct((t, 2 * d), BF16),
    )
    row = lambda i: (i, 0)
    hrow = lambda i: (0, i, 0)
    out_specs = (
        pl.BlockSpec((tm, 512), row),
        pl.BlockSpec((tm, 128), row),
        pl.BlockSpec((tm, 128), row),
        pl.BlockSpec((DIFF_HEADS, tm, 128), hrow),
        pl.BlockSpec((DIFF_HEADS, tm, 128), hrow),
        pl.BlockSpec((DIFF_HEADS, tm, 128), hrow),
        pl.BlockSpec((tm, 2 * d), row),
    )
    lconst = lambda i: (layer, 0, 0)
    in_specs = [
        pl.BlockSpec((tm, d), row),
        pl.BlockSpec((None, 1, d), lconst),
        pl.BlockSpec((None, d, ncols), lconst, pipeline_mode=pl.Buffered(1)),
        pl.BlockSpec((None, 1, gcol.shape[2]), lconst),
        pl.BlockSpec(bd.shape, const),
        pl.BlockSpec((None, 1, 2 * d), lconst),
    ]
    nbytes = (2 * tm * d * 4 + d * ncols * 2 + 2 * tm * ncols * 2 + tm * d * 2
              + 3 * tm * PROJ_NORMED * 4)
    return pl.pallas_call(
        _proj_kernel,
        grid=grid,
        in_specs=in_specs,
        out_specs=out_specs,
        out_shape=out_shape,
        scratch_shapes=[pltpu.VMEM((tm, d), BF16), pltpu.VMEM((tm, PROJ_NORMED), F32)],
        compiler_params=pltpu.CompilerParams(
            dimension_semantics=("arbitrary",), vmem_limit_bytes=_vmem_limit(nbytes)),
        name="proj",
    )(x, gmix, w, gcol, bd, bg)


SWA_FILL = 512
MASKED_LOGIT_SHIFT = 1e30


def _swa_fixed_kernel(scal_ref, q_ref, k_ref, v_ref, o_ref, kbuf_ref, vbuf_ref, tab_ref,
                      *, tq, seq):
    b = pl.program_id(0)
    i = pl.program_id(1)
    nblk = tq // BLOCK
    rows = SWA_GROUP * BLOCK
    win = 2 * BLOCK

    @pl.when(jnp.logical_and(b == 0, i == 0))
    def _build_tables():
        row = lax.broadcasted_iota(jnp.int32, (rows, win), 0)
        col = lax.broadcasted_iota(jnp.int32, (rows, win), 1)
        head = lax.shift_right_logical(row, 7)
        dist = jnp.bitwise_and(row, BLOCK - 1) + BLOCK - col
        valid = jnp.logical_and(dist >= 0, dist < BLOCK)
        has_key = jnp.logical_and(valid, col >= BLOCK)
        distf = dist.astype(F32)
        for g in range(SWA_KV_HEADS):
            def per_head(base):
                h0 = SWA_GROUP * g
                return jnp.where(head == 0, scal_ref[base + h0],
                                 jnp.where(head == 1, scal_ref[base + h0 + 1],
                                           jnp.where(head == 2, scal_ref[base + h0 + 2],
                                                     scal_ref[base + h0 + 3])))
            t = per_head(0) * distf + per_head(2 * SWA_Q_HEADS)
            tab_ref[g] = jnp.where(valid, t, MASKED_LOGIT_SHIFT)
            tab_ref[SWA_KV_HEADS + g] = jnp.where(has_key, t, MASKED_LOGIT_SHIFT)

    @pl.when(i == 0)
    def _stage_kv():
        kbuf_ref[:, 0:BLOCK, :] = jnp.zeros((SWA_KV_HEADS, BLOCK, LANES), BF16)
        vbuf_ref[:, 0:BLOCK, :] = jnp.zeros((SWA_KV_HEADS, BLOCK, 2 * LANES), BF16)
        lane = lax.broadcasted_iota(jnp.int32, (SWA_FILL, LANES), 1)
        first = lane < HEAD_DIM

        def body(c, carry):
            r0 = pl.multiple_of(c * SWA_FILL, SWA_FILL)
            dst = pl.ds(pl.multiple_of(r0 + BLOCK, BLOCK), SWA_FILL)
            for src_ref, dst_ref in ((k_ref, kbuf_ref), (v_ref, vbuf_ref)):
                x = src_ref[pl.ds(r0, SWA_FILL), :].astype(F32)
                xr = pltpu.roll(x, HEAD_DIM, 1)
                dst_ref[0, dst, 0:LANES] = jnp.where(first, x, xr).astype(BF16)
                dst_ref[1, dst, 0:LANES] = jnp.where(first, xr, x).astype(BF16)
            vbuf_ref[:, dst, LANES:2 * LANES] = jnp.ones((SWA_KV_HEADS, SWA_FILL, LANES), BF16)
            return carry
        lax.fori_loop(0, seq // SWA_FILL, body, 0)

    lane = lax.broadcasted_iota(jnp.int32, (BLOCK, LANES), 1)
    first = lane < HEAD_DIM
    for r in range(nblk):
        n = i * nblk + r
        w0 = pl.multiple_of(n * BLOCK, BLOCK)
        tsel = jnp.where(n == 0, SWA_KV_HEADS, 0)
        rs = slice(r * BLOCK, (r + 1) * BLOCK)
        for g in range(SWA_KV_HEADS):
            slabs = [q_ref[rs, (2 * g + a) * LANES:(2 * g + a + 1) * LANES] for a in range(2)]
            zero = jnp.zeros_like(slabs[0])
            lhs = jnp.concatenate(
                [jnp.where(first, slabs[0], zero), jnp.where(first, zero, slabs[0]),
                 jnp.where(first, slabs[1], zero), jnp.where(first, zero, slabs[1])], axis=0)
            s = _dot_nt(lhs, kbuf_ref[g, pl.ds(w0, win), :])
            p = jnp.exp2(s - tab_ref[tsel + g]).astype(BF16)
            out = _dot(p, vbuf_ref[g, pl.ds(w0, win), :])
            for a in range(2):
                halves = []
                for k in range(2):
                    hq = SWA_GROUP * g + 2 * a + k
                    blk = out[(2 * a + k) * BLOCK:(2 * a + k + 1) * BLOCK]
                    denom = blk[:, LANES:2 * LANES] + scal_ref[3 * SWA_Q_HEADS + hq]
                    halves.append(blk[:, 0:LANES] / denom)
                o_ref[rs, (2 * g + a) * LANES:(2 * g + a + 1) * LANES] = (
                    jnp.where(first, halves[0], halves[1]).astype(BF16))


def _swa_online_kernel(scal_ref, q_ref, k_ref, v_ref, o_ref, *, tq):
    i = pl.program_id(1)
    qi = lax.broadcasted_iota(jnp.int32, (BLOCK, BLOCK), 0)
    kj = lax.broadcasted_iota(jnp.int32, (BLOCK, BLOCK), 1)
    lower = kj <= qi
    dist = jnp.where(lower, qi - kj, BLOCK + qi - kj).astype(F32)
    for r in range(tq // BLOCK):
        n = i * (tq // BLOCK) + r
        cur = pl.multiple_of(n * BLOCK, BLOCK)
        prev = pl.multiple_of(jnp.maximum(n - 1, 0) * BLOCK, BLOCK)
        valid = jnp.logical_or(lower, n > 0)
        k_cur = k_ref[pl.ds(cur, BLOCK), :]
        k_prev = k_ref[pl.ds(prev, BLOCK), :]
        v_cur = v_ref[pl.ds(cur, BLOCK), :]
        v_prev = v_ref[pl.ds(prev, BLOCK), :]
        outs = []
        for hq in range(SWA_Q_HEADS):
            g = hq // SWA_GROUP
            gs = slice(g * HEAD_DIM, (g + 1) * HEAD_DIM)
            qh = q_ref[r * BLOCK:(r + 1) * BLOCK, hq * HEAD_DIM:(hq + 1) * HEAD_DIM]
            s_cur = _dot_nt(qh, k_cur[:, gs])
            s_prev = _dot_nt(qh, k_prev[:, gs])
            s = jnp.where(lower, s_cur, s_prev) - scal_ref[hq] * dist
            s = jnp.where(valid, s, NEG)
            sink = scal_ref[SWA_Q_HEADS + hq]
            m = jnp.maximum(jnp.max(s, axis=-1, keepdims=True), sink)
            e = jnp.exp2(s - m)
            denom = jnp.sum(e, axis=-1, keepdims=True) + jnp.exp2(sink - m)
            p = e / denom
            p_cur = jnp.where(lower, p, 0.0).astype(BF16)
            p_prev = jnp.where(lower, 0.0, p).astype(BF16)
            outs.append(_dot(p_cur, v_cur[:, gs]) + _dot(p_prev, v_prev[:, gs]))
        o_ref[r * BLOCK:(r + 1) * BLOCK, :] = jnp.concatenate(outs, axis=-1).astype(BF16)


def _swa(qa, ka, va, scal, *, bsz, seq, tq, fixed_reference):
    nq = seq // tq
    nbytes = 2 * (2 * tq * 512 * 2 + 2 * seq * LANES * 2)
    if fixed_reference:
        kernel = functools.partial(_swa_fixed_kernel, tq=tq, seq=seq)
        rows = SWA_GROUP * BLOCK
        scratch = [
            pltpu.VMEM((SWA_KV_HEADS, seq + BLOCK, LANES), BF16),
            pltpu.VMEM((SWA_KV_HEADS, seq + BLOCK, 2 * LANES), BF16),
            pltpu.VMEM((2 * SWA_KV_HEADS, rows, 2 * BLOCK), F32),
        ]
        nbytes += (SWA_KV_HEADS * (seq + BLOCK) * 3 * LANES * 2 + 2 * SWA_KV_HEADS * rows * 2 * BLOCK * 4
                   + 8 * rows * 2 * BLOCK * 4)
        name = "swa"
    else:
        kernel = functools.partial(_swa_online_kernel, tq=tq)
        scratch = []
        name = "swa_online"
    return pl.pallas_call(
        kernel,
        grid=(bsz, nq),
        in_specs=[
            pl.BlockSpec(memory_space=pltpu.SMEM),
            pl.BlockSpec((tq, 512), lambda b, i: (b * nq + i, 0)),
            pl.BlockSpec((seq, LANES), lambda b, i: (b, 0)),
            pl.BlockSpec((seq, LANES), lambda b, i: (b, 0)),
        ],
        out_specs=pl.BlockSpec((tq, 512), lambda b, i: (b * nq + i, 0)),
        out_shape=jax.ShapeDtypeStruct(qa.shape, BF16),
        scratch_shapes=scratch,
        compiler_params=pltpu.CompilerParams(
            dimension_semantics=("arbitrary", "arbitrary"), vmem_limit_bytes=_vmem_limit(nbytes)),
        name=name,
    )(scal, qa, ka, va)


def _stack_q(q, tq):
    lane = lax.broadcasted_iota(jnp.int32, (tq, LANES), 1)
    zero = jnp.zeros_like(q)
    return jnp.where(lane < HEAD_DIM, q, zero), jnp.where(lane < HEAD_DIM, zero, q)


def _diff_finish(o1, o2, lam_init, lp_ref, g_ref):
    lp = lp_ref[...]
    lam = (jnp.exp(jnp.sum(lp[0:1] * lp[1:2], axis=-1, keepdims=True))
           - jnp.exp(jnp.sum(lp[2:3] * lp[3:4], axis=-1, keepdims=True)) + lam_init)
    o = o1 - lam * o2
    ms = jnp.mean(o * o, axis=-1, keepdims=True)
    return (o * lax.rsqrt(ms + EPS) * g_ref[...] * (1.0 - lam_init)).astype(BF16)


def _causal_keep(i, tq, start, width):
    col = start + lax.broadcasted_iota(jnp.int32, (1, width), 1)
    row = i * tq + lax.rem(lax.broadcasted_iota(jnp.int32, (2 * tq, 1), 0), tq)
    return col <= row


AUG_K_BASE = 3
AUG_Q_BASE = 9
DIFF_PART_SHIFT = 9
DIFF_PART = 1 << DIFF_PART_SHIFT
DIFF_KEY_BLOCK = 1024


def _split3(val):
    hi = val.astype(BF16).astype(F32)
    r1 = val - hi
    mid = r1.astype(BF16).astype(F32)
    lo = (r1 - mid).astype(BF16).astype(F32)
    return hi, mid, lo


def _aug_templates(slope2, rows, tmpl_ref):
    r = lax.broadcasted_iota(jnp.int32, (rows, LANES), 0).astype(F32)
    hi, mid, lo = _split3(slope2 * r)
    lane = lax.broadcasted_iota(jnp.int32, (rows, LANES), 1)
    parts = jnp.where(jnp.logical_or(lane == 0, lane == 6), hi,
                      jnp.where(jnp.logical_or(lane == 1, lane == 7), mid, lo))
    k_ones = jnp.logical_and(lane >= 6, lane < 12)
    tmpl_ref[0] = jnp.where(lane < 3, parts, jnp.where(k_ones, 1.0, 0.0))
    tmpl_ref[1] = jnp.where(lane < 6, 1.0, jnp.where(lane < 9, -parts, 0.0))


def _with_scalar_lanes(template, value, base):
    hi, mid, lo = _split3(jnp.full((1, LANES), value, F32))
    lane = lax.broadcasted_iota(jnp.int32, (1, LANES), 1)
    terms = jnp.where(lane == base, hi, jnp.where(lane == base + 1, mid, lo))
    here = jnp.logical_and(lane >= base, lane < base + 3)
    return jnp.where(here, terms, template).astype(BF16)


def _diff_fixed_kernel(scal_ref, q_ref, k_ref, v_ref, lp_ref, g_ref, o_ref,
                       kaug_ref, vaug_ref, qs_ref, acc_ref, tmpl_ref, *, tq, seq):
    h = pl.program_id(1)
    i = pl.program_id(2)
    lam_init = scal_ref[0]
    bound2 = scal_ref[1]
    slope2 = scal_ref[2 + h]
    part = DIFF_PART
    nparts = tq // part

    @pl.when(i == 0)
    def _build_kv():
        _aug_templates(slope2, part, tmpl_ref)

        def body(c, carry):
            r0 = pl.multiple_of(c * part, part)
            kaug_ref[pl.ds(r0, part), 0:LANES] = k_ref[pl.ds(r0, part), :]
            kaug_ref[pl.ds(r0, part), LANES:2 * LANES] = _with_scalar_lanes(
                tmpl_ref[0], slope2 * r0.astype(F32), AUG_K_BASE)
            vaug_ref[pl.ds(r0, part), 0:LANES] = v_ref[pl.ds(r0, part), :]
            vaug_ref[pl.ds(r0, part), LANES:2 * LANES] = jnp.ones((part, LANES), BF16)
            return carry
        lax.fori_loop(0, seq // part, body, 0)

    for c in range(nparts):
        q1, q2 = _stack_q(q_ref[c * part:(c + 1) * part, :], part)
        row0 = (i * tq + c * part).astype(F32)
        aug = _with_scalar_lanes(tmpl_ref[1], -(slope2 * row0 + bound2), AUG_Q_BASE)
        base = 2 * part * c
        qs_ref[base:base + part, 0:LANES] = q1
        qs_ref[base + part:base + 2 * part, 0:LANES] = q2
        qs_ref[base:base + part, LANES:2 * LANES] = aug
        qs_ref[base + part:base + 2 * part, LANES:2 * LANES] = aug
    acc_ref[...] = jnp.zeros(acc_ref.shape, F32)

    def full_block(j):
        start = pl.multiple_of(j * DIFF_KEY_BLOCK, DIFF_KEY_BLOCK)
        p = jnp.exp2(_dot_nt(qs_ref[...], kaug_ref[pl.ds(start, DIFF_KEY_BLOCK), :]))
        acc_ref[...] += _dot(p.astype(BF16), vaug_ref[pl.ds(start, DIFF_KEY_BLOCK), :])

    def block_pair(j, carry):
        full_block(2 * j)
        full_block(2 * j + 1)
        return carry

    lax.fori_loop(0, i * (tq // (2 * DIFF_KEY_BLOCK)), block_pair, 0)

    col = lax.broadcasted_iota(jnp.int32, (1, part), 1)
    for c in range(nparts):
        start = pl.multiple_of(i * tq + c * part, part)
        lo = 2 * part * c
        srow = lax.broadcasted_iota(jnp.int32, (2 * tq - lo, 1), 0)
        row = (c + lax.shift_right_logical(srow, DIFF_PART_SHIFT + 1)) * part + jnp.bitwise_and(srow, part - 1)
        p = jnp.exp2(_dot_nt(qs_ref[lo:2 * tq, :], kaug_ref[pl.ds(start, part), :]))
        p = jnp.where(col + c * part <= row, p, 0.0).astype(BF16)
        acc_ref[lo:2 * tq, :] += _dot(p, vaug_ref[pl.ds(start, part), :])

    for c in range(nparts):
        base = 2 * part * c
        a1 = acc_ref[base:base + part, :]
        a2 = acc_ref[base + part:base + 2 * part, :]
        o_ref[c * part:(c + 1) * part, :] = _diff_finish(
            a1[:, 0:LANES] / a1[:, LANES:2 * LANES], a2[:, 0:LANES] / a2[:, LANES:2 * LANES],
            lam_init, lp_ref, g_ref)


def _diff_online_kernel(scal_ref, q_ref, k_ref, v_ref, lp_ref, g_ref, o_ref,
                        qs_ref, m_ref, l_ref, acc_ref, *, tq):
    h = pl.program_id(1)
    i = pl.program_id(2)
    lam_init = scal_ref[0]
    slope2 = scal_ref[2 + h]

    q1, q2 = _stack_q(q_ref[...], tq)
    qs_ref[0:tq, :] = q1
    qs_ref[tq:2 * tq, :] = q2
    m_ref[...] = jnp.full(m_ref.shape, NEG, F32)
    l_ref[...] = jnp.zeros(l_ref.shape, F32)
    acc_ref[...] = jnp.zeros(acc_ref.shape, F32)

    def step(j, masked):
        start = pl.multiple_of(j * tq, tq)
        s = _dot_nt(qs_ref[...], k_ref[pl.ds(start, tq), :])
        col = start + lax.broadcasted_iota(jnp.int32, (1, tq), 1)
        s = s + slope2 * col.astype(F32)
        if masked:
            s = jnp.where(_causal_keep(i, tq, start, tq), s, NEG)
        m_prev = m_ref[...]
        m_new = jnp.maximum(m_prev, jnp.max(s, axis=-1, keepdims=True))
        alpha = jnp.exp2(m_prev - m_new)
        p = jnp.exp2(s - m_new)
        l_ref[...] = alpha * l_ref[...] + jnp.sum(p, axis=-1, keepdims=True)
        acc_ref[...] = alpha * acc_ref[...] + _dot(p.astype(BF16), v_ref[pl.ds(start, tq), :])
        m_ref[...] = m_new

    def body(j, carry):
        step(j, False)
        return carry

    lax.fori_loop(0, i, body, 0)
    step(i, True)

    acc = acc_ref[...]
    l = l_ref[...]
    o_ref[...] = _diff_finish(acc[0:tq] / l[0:tq], acc[tq:2 * tq] / l[tq:2 * tq],
                              lam_init, lp_ref, g_ref)


def _diff(qd, kd, vd, scal, lp, gsub, *, layer, bsz, seq, tq, fixed_reference):
    nq = seq // tq
    t = bsz * seq
    if fixed_reference:
        assert tq % (2 * DIFF_KEY_BLOCK) == 0 and seq % tq == 0
        kernel = functools.partial(_diff_fixed_kernel, tq=tq, seq=seq)
        scratch = [
            pltpu.VMEM((seq, 2 * LANES), BF16),
            pltpu.VMEM((seq, 2 * LANES), BF16),
            pltpu.VMEM((2 * tq, 2 * LANES), BF16),
            pltpu.VMEM((2 * tq, 2 * LANES), F32),
            pltpu.VMEM((2, DIFF_PART, LANES), F32),
        ]
        scratch_bytes = (2 * seq * 2 * LANES * 2 + 2 * tq * 2 * LANES * 6 + 2 * DIFF_PART * LANES * 4
                         + 2 * tq * DIFF_KEY_BLOCK * 6)
        name = "diff"
    else:
        kernel = functools.partial(_diff_online_kernel, tq=tq)
        scratch = [
            pltpu.VMEM((2 * tq, LANES), BF16),
            pltpu.VMEM((2 * tq, 1), F32),
            pltpu.VMEM((2 * tq, 1), F32),
            pltpu.VMEM((2 * tq, LANES), F32),
        ]
        scratch_bytes = 2 * tq * (LANES * 6 + 8) + 6 * 2 * tq * tq * 4
        name = "diff_online"
    nbytes = 2 * (2 * seq * LANES * 2 + 2 * tq * LANES * 2) + scratch_bytes
    return pl.pallas_call(
        kernel,
        grid=(bsz, DIFF_HEADS, nq),
        in_specs=[
            pl.BlockSpec(memory_space=pltpu.SMEM),
            pl.BlockSpec((None, tq, LANES), lambda b, h, i: (h, b * nq + i, 0)),
            pl.BlockSpec((None, seq, LANES), lambda b, h, i: (h, b, 0)),
            pl.BlockSpec((None, seq, LANES), lambda b, h, i: (h, b, 0)),
            pl.BlockSpec((None, 4, HEAD_DIM), lambda b, h, i: (layer, 0, 0)),
            pl.BlockSpec((None, 1, DIFF_V_DIM), lambda b, h, i: (layer, 0, 0)),
        ],
        out_specs=pl.BlockSpec((tq, LANES), lambda b, h, i: (b * nq + i, h)),
        out_shape=jax.ShapeDtypeStruct((t, DIFF_HEADS * DIFF_V_DIM), BF16),
        scratch_shapes=scratch,
        compiler_params=pltpu.CompilerParams(
            dimension_semantics=("arbitrary", "arbitrary", "arbitrary"),
            vmem_limit_bytes=_vmem_limit(nbytes)),
        name=name,
    )(scal, qd, kd, vd, lp, gsub)


def _mixffn_kernel(x_ref, oa_ref, ob_ref, gate_ref, wb_ref, wo_ref, gn_ref, win_ref, wout_ref,
                   out_ref, h_ref, *, chunk):
    d = x_ref.shape[1]
    up_a = _dot(oa_ref[...], wb_ref[0])
    up_b = _dot(ob_ref[...], wb_ref[1])
    merged = gate_ref[:, :d].astype(F32) * up_a + gate_ref[:, d:].astype(F32) * up_b
    x = x_ref[...] + _dot(merged.astype(BF16), wo_ref[...])
    ms = jnp.mean(x * x, axis=-1, keepdims=True)
    h_ref[...] = (x * lax.rsqrt(ms + EPS) * gn_ref[...]).astype(BF16)
    hidden = wout_ref.shape[0]
    acc = x
    for c in range(hidden // chunk):
        g = _dot(h_ref[...], win_ref[:, c * chunk:(c + 1) * chunk])
        u = _dot(h_ref[...], win_ref[:, hidden + c * chunk:hidden + (c + 1) * chunk])
        a = (g * jax.nn.sigmoid(g) * u).astype(BF16)
        acc = acc + _dot(a, wout_ref[c * chunk:(c + 1) * chunk, :])
    out_ref[...] = acc


def _mixffn(x, oa, ob, gates, wb, wo, gn, win, wout, *, layer, tm, chunk):
    t, d = x.shape
    hidden = wout.shape[1]
    row = lambda i: (i, 0)
    lconst = lambda i: (layer, 0, 0)
    resident = dict(pipeline_mode=pl.Buffered(1))
    kernel = functools.partial(_mixffn_kernel, chunk=chunk)
    weight_bytes = (2 * 512 * d + d * d + 3 * d * hidden) * 2
    nbytes = weight_bytes + 2 * (2 * tm * d * 4 + 2 * tm * 512 * 2 + tm * 2 * d * 2) + tm * d * 10
    return pl.pallas_call(
        kernel,
        grid=(t // tm,),
        in_specs=[
            pl.BlockSpec((tm, d), row),
            pl.BlockSpec((tm, 512), row),
            pl.BlockSpec((tm, 512), row),
            pl.BlockSpec((tm, 2 * d), row),
            pl.BlockSpec((None,) + wb.shape[1:], lambda i: (layer, 0, 0, 0), **resident),
            pl.BlockSpec((None, d, d), lconst, **resident),
            pl.BlockSpec((None, 1, d), lconst),
            pl.BlockSpec((None, d, 2 * hidden), lconst, **resident),
            pl.BlockSpec((None, hidden, d), lconst, **resident),
        ],
        out_specs=pl.BlockSpec((tm, d), row),
        out_shape=jax.ShapeDtypeStruct(x.shape, F32),
        scratch_shapes=[pltpu.VMEM((tm, d), BF16)],
        compiler_params=pltpu.CompilerParams(
            dimension_semantics=("arbitrary",), vmem_limit_bytes=_vmem_limit(nbytes)),
        name="mixffn",
    )(x, oa, ob, gates, wb, wo, gn, win, wout)


def _alibi_slopes():
    return jnp.exp2(-8.0 * jnp.arange(1, N_ALIBI_HEADS + 1, dtype=F32) / N_ALIBI_HEADS)


def _group_mean_matrix():
    idx = jnp.arange(256) // HEAD_DIM
    return jnp.where(idx[:, None] == idx[None, :], 1.0 / HEAD_DIM, 0.0).astype(BF16)


def kernel(x, w_in, b_gate, w_branch, w_o, norm_mix, norm_ffn, qk_norm_swa, qk_norm_diff,
           attn_sinks, diff_lambda, diff_subln, w_ffn_in, w_ffn_out):
    bsz, seq, d = x.shape
    depth = w_in.shape[0]
    t = bsz * seq
    slopes = _alibi_slopes()
    slopes_swa, slopes_diff = slopes[:SWA_Q_HEADS], slopes[SWA_Q_HEADS:]
    bd = _group_mean_matrix()

    w_in_b, w_branch_b, w_o_b = w_in.astype(BF16), w_branch.astype(BF16), w_o.astype(BF16)
    w_ffn_in_b, w_ffn_out_b = w_ffn_in.astype(BF16), w_ffn_out.astype(BF16)
    gmix = norm_mix.astype(F32).reshape(depth, 1, d)
    gffn = norm_ffn.astype(F32).reshape(depth, 1, d)
    bg = b_gate.astype(F32).reshape(depth, 1, 2 * d)
    lp = diff_lambda.astype(F32)
    gsub = diff_subln.astype(F32).reshape(depth, 1, DIFF_V_DIM)
    q_fold = QK_SCALE * LOG2E
    gcol = jnp.concatenate(
        [jnp.tile(qk_norm_swa[:, 0], (1, SWA_Q_HEADS)) * q_fold,
         jnp.tile(qk_norm_swa[:, 1], (1, SWA_KV_HEADS)),
         jnp.ones((depth, SWA_KV_HEADS * HEAD_DIM), F32),
         jnp.tile(qk_norm_diff[:, 0], (1, 2 * DIFF_HEADS)) * q_fold,
         jnp.tile(qk_norm_diff[:, 1], (1, 2 * DIFF_HEADS))], axis=1).astype(F32).reshape(depth, 1, -1)

    def score_bound(qk_norm):
        gains = jnp.max(jnp.abs(qk_norm.astype(F32)), axis=-1)
        return SCORE_BOUND_PER_GAIN * gains[:, 0] * gains[:, 1]

    bound_a, bound_d = score_bound(qk_norm_swa), score_bound(qk_norm_diff)
    sinks2 = attn_sinks.astype(F32) * LOG2E
    ref_a = jnp.maximum(bound_a[:, None] * LOG2E, sinks2)
    scal_a = jnp.concatenate(
        [jnp.broadcast_to(slopes_swa * LOG2E, (depth, SWA_Q_HEADS)), sinks2, ref_a,
         jnp.exp2(sinks2 - ref_a)], axis=1)
    lam_init = jnp.array([0.8 - 0.6 * math.exp(-0.3 * l) for l in range(depth)], F32)
    scal_d = jnp.concatenate(
        [lam_init[:, None], bound_d[:, None] * LOG2E,
         jnp.broadcast_to(slopes_diff * LOG2E, (depth, DIFF_HEADS))], axis=1)

    xf = x.reshape(t, d)
    for l in range(depth):
        qa, ka, va, qd, kd, vd, gates = _proj(xf, gmix, w_in_b, gcol, bd, bg, layer=l, tm=1024)

        o_a = lax.cond(
            bound_a[l] <= FIXED_REF_MAX_BOUND,
            lambda a: _swa(*a, bsz=bsz, seq=seq, tq=2048, fixed_reference=True),
            lambda a: _swa(*a, bsz=bsz, seq=seq, tq=512, fixed_reference=False),
            (qa, ka, va, scal_a[l]))

        o_b = lax.cond(
            bound_d[l] <= FIXED_REF_MAX_BOUND,
            lambda a: _diff(*a, layer=l, bsz=bsz, seq=seq, tq=2048, fixed_reference=True),
            lambda a: _diff(*a, layer=l, bsz=bsz, seq=seq, tq=256, fixed_reference=False),
            (qd, kd, vd, scal_d[l], lp, gsub))

        xf = _mixffn(xf, o_a, o_b, gates, w_branch_b, w_o_b, gffn, w_ffn_in_b, w_ffn_out_b,
                     layer=l, tm=512, chunk=256)
    return xf.reshape(bsz, seq, d)
```

```python
import functools
import math

import jax
import jax.numpy as jnp
from jax import lax
from jax.experimental import pallas as pl
from jax.experimental.pallas import tpu as pltpu

F32 = jnp.float32
BF16 = jnp.bfloat16

HEAD_DIM = 64
BLOCK = 128
SWA_Q_HEADS = 8
SWA_KV_HEADS = 2
SWA_GROUP = SWA_Q_HEADS // SWA_KV_HEADS
DIFF_HEADS = 4
DIFF_V_DIM = 2 * HEAD_DIM
N_ALIBI_HEADS = SWA_Q_HEADS + DIFF_HEADS
NEG = -1e30
EPS = 1e-6
QK_SCALE = HEAD_DIM ** -0.5
LOG2E = math.log2(math.e)
SCORE_BOUND_PER_GAIN = HEAD_DIM * QK_SCALE
FIXED_REF_MAX_BOUND = 20.0

V7X_VMEM_BYTES = 64 * 1024 * 1024
LANES = 128
BRANCH_WIDTH = SWA_Q_HEADS * HEAD_DIM
SWA_KV_WIDTH = SWA_KV_HEADS * HEAD_DIM

PROJ_ROWS = 1024
MIXFFN_ROWS = 512
FFN_CHUNK = 256
SWA_QUERIES = 2048
SWA_ONLINE_QUERIES = 512
DIFF_QUERIES = 2048
DIFF_ONLINE_QUERIES = 256


def _vmem_limit(nbytes):
    return int(min(max(2 * nbytes, 32 * 1024 * 1024), V7X_VMEM_BYTES - 8 * 1024 * 1024))


def _dot(a, b):
    return jnp.dot(a, b, preferred_element_type=F32)


def _dot_nt(a, b):
    return lax.dot_general(a, b, (((1,), (1,)), ((), ())), preferred_element_type=F32)


PROJ_CHUNK = 256
PROJ_NORMED = 1792


def _proj_kernel(x_ref, gmix_ref, w_ref, gcol_ref, bd_ref, bg_ref,
                 qa_ref, ka_ref, va_ref, qd_ref, kd_ref, vd_ref, gate_ref, h_ref, y_ref):
    tm = x_ref.shape[0]
    x = x_ref[...]
    ms = jnp.mean(x * x, axis=-1, keepdims=True)
    h_ref[...] = (x * lax.rsqrt(ms + EPS) * gmix_ref[...]).astype(BF16)

    def proj(c0, width):
        return _dot(h_ref[...], w_ref[:, c0:c0 + width])

    nchunk = PROJ_NORMED // PROJ_CHUNK
    y_ref[...] = proj(0, PROJ_NORMED)
    sq = jnp.concatenate(
        [jnp.square(y_ref[:, PROJ_CHUNK * c:PROJ_CHUNK * (c + 1)]).astype(BF16)
         for c in range(nchunk)], axis=0)
    inv = lax.rsqrt(_dot(sq, bd_ref[...]) + EPS)

    def normed(c):
        cs = slice(PROJ_CHUNK * c, PROJ_CHUNK * (c + 1))
        return (y_ref[:, cs] * inv[c * tm:(c + 1) * tm] * gcol_ref[:, cs]).astype(BF16)

    for c in range(2):
        qa_ref[:, PROJ_CHUNK * c:PROJ_CHUNK * (c + 1)] = normed(c)
    ka_ref[...] = normed(2)[:, :LANES]
    va_ref[...] = y_ref[:, 2 * PROJ_CHUNK + LANES:3 * PROJ_CHUNK].astype(BF16)
    for c in range(2):
        y = normed(3 + c)
        qd_ref[2 * c] = y[:, :LANES]
        qd_ref[2 * c + 1] = y[:, LANES:]
    for c in range(2):
        y = normed(5 + c)
        kd_ref[2 * c] = y[:, :LANES]
        kd_ref[2 * c + 1] = y[:, LANES:]
    y = proj(PROJ_NORMED, DIFF_HEADS * LANES).astype(BF16)
    for hd in range(DIFF_HEADS):
        vd_ref[hd] = y[:, hd * LANES:(hd + 1) * LANES]
    g0 = PROJ_NORMED + DIFF_HEADS * LANES
    gate_ref[...] = jax.nn.sigmoid(proj(g0, gate_ref.shape[1]) + bg_ref[...]).astype(BF16)


def _proj(x, gmix, w, gcol, bd, bg, *, layer, tm):
    t, d = x.shape
    ncols = w.shape[2]
    assert t % tm == 0
    assert ncols == BRANCH_WIDTH + 2 * SWA_KV_WIDTH + 3 * DIFF_HEADS * DIFF_V_DIM + 2 * d
    assert PROJ_NORMED == BRANCH_WIDTH + 2 * SWA_KV_WIDTH + 2 * DIFF_HEADS * DIFF_V_DIM
    grid = (t // tm,)
    const = lambda i: (0, 0)
    out_shape = (
        jax.ShapeDtypeStruct((t, BRANCH_WIDTH), BF16),
        jax.ShapeDtypeStruct((t, SWA_KV_WIDTH), BF16),
        jax.ShapeDtypeStruct((t, SWA_KV_WIDTH), BF16),
        jax.ShapeDtypeStruct((DIFF_HEADS, t, DIFF_V_DIM), BF16),
        jax.ShapeDtypeStruct((DIFF_HEADS, t, DIFF_V_DIM), BF16),
        jax.ShapeDtypeStruct((DIFF_HEADS, t, DIFF_V_DIM), BF16),
        jax.ShapeDtypeStruct((t, 2 * d), BF16),
    )
    row = lambda i: (i, 0)
    hrow = lambda i: (0, i, 0)
    out_specs = (
        pl.BlockSpec((tm, BRANCH_WIDTH), row),
        pl.BlockSpec((tm, SWA_KV_WIDTH), row),
        pl.BlockSpec((tm, SWA_KV_WIDTH), row),
        pl.BlockSpec((DIFF_HEADS, tm, DIFF_V_DIM), hrow),
        pl.BlockSpec((DIFF_HEADS, tm, DIFF_V_DIM), hrow),
        pl.BlockSpec((DIFF_HEADS, tm, DIFF_V_DIM), hrow),
        pl.BlockSpec((tm, 2 * d), row),
    )
    lconst = lambda i: (layer, 0, 0)
    in_specs = [
        pl.BlockSpec((tm, d), row),
        pl.BlockSpec((None, 1, d), lconst),
        pl.BlockSpec((None, d, ncols), lconst, pipeline_mode=pl.Buffered(1)),
        pl.BlockSpec((None, 1, gcol.shape[2]), lconst),
        pl.BlockSpec(bd.shape, const),
        pl.BlockSpec((None, 1, 2 * d), lconst),
    ]
    nbytes = (2 * tm * d * 4 + d * ncols * 2 + 2 * tm * ncols * 2 + tm * d * 2
              + 3 * tm * PROJ_NORMED * 4)
    return pl.pallas_call(
        _proj_kernel,
        grid=grid,
        in_specs=in_specs,
        out_specs=out_specs,
        out_shape=out_shape,
        scratch_shapes=[pltpu.VMEM((tm, d), BF16), pltpu.VMEM((tm, PROJ_NORMED), F32)],
        compiler_params=pltpu.CompilerParams(
            dimension_semantics=("arbitrary",), vmem_limit_bytes=_vmem_limit(nbytes)),
        name="proj",
    )(x, gmix, w, gcol, bd, bg)


SWA_FILL = 512
MASKED_LOGIT_SHIFT = 1e30


def _swa_fixed_kernel(scal_ref, q_ref, k_ref, v_ref, o_ref, kbuf_ref, vbuf_ref, tab_ref,
                      *, tq, seq):
    b = pl.program_id(0)
    i = pl.program_id(1)
    nblk = tq // BLOCK
    rows = SWA_GROUP * BLOCK
    win = 2 * BLOCK

    @pl.when(jnp.logical_and(b == 0, i == 0))
    def _build_tables():
        row = lax.broadcasted_iota(jnp.int32, (rows, win), 0)
        col = lax.broadcasted_iota(jnp.int32, (rows, win), 1)
        head = lax.shift_right_logical(row, 7)
        dist = jnp.bitwise_and(row, BLOCK - 1) + BLOCK - col
        valid = jnp.logical_and(dist >= 0, dist < BLOCK)
        has_key = jnp.logical_and(valid, col >= BLOCK)
        distf = dist.astype(F32)
        for g in range(SWA_KV_HEADS):
            def per_head(base):
                h0 = SWA_GROUP * g
                return jnp.where(head == 0, scal_ref[base + h0],
                                 jnp.where(head == 1, scal_ref[base + h0 + 1],
                                           jnp.where(head == 2, scal_ref[base + h0 + 2],
                                                     scal_ref[base + h0 + 3])))
            t = per_head(0) * distf + per_head(2 * SWA_Q_HEADS)
            tab_ref[g] = jnp.where(valid, t, MASKED_LOGIT_SHIFT)
            tab_ref[SWA_KV_HEADS + g] = jnp.where(has_key, t, MASKED_LOGIT_SHIFT)

    @pl.when(i == 0)
    def _stage_kv():
        kbuf_ref[:, 0:BLOCK, :] = jnp.zeros((SWA_KV_HEADS, BLOCK, LANES), BF16)
        vbuf_ref[:, 0:BLOCK, :] = jnp.zeros((SWA_KV_HEADS, BLOCK, 2 * LANES), BF16)
        lane = lax.broadcasted_iota(jnp.int32, (SWA_FILL, LANES), 1)
        first = lane < HEAD_DIM

        def body(c, carry):
            r0 = pl.multiple_of(c * SWA_FILL, SWA_FILL)
            dst = pl.ds(pl.multiple_of(r0 + BLOCK, BLOCK), SWA_FILL)
            for src_ref, dst_ref in ((k_ref, kbuf_ref), (v_ref, vbuf_ref)):
                x = src_ref[pl.ds(r0, SWA_FILL), :].astype(F32)
                xr = pltpu.roll(x, HEAD_DIM, 1)
                dst_ref[0, dst, 0:LANES] = jnp.where(first, x, xr).astype(BF16)
                dst_ref[1, dst, 0:LANES] = jnp.where(first, xr, x).astype(BF16)
            vbuf_ref[:, dst, LANES:2 * LANES] = jnp.ones((SWA_KV_HEADS, SWA_FILL, LANES), BF16)
            return carry
        lax.fori_loop(0, seq // SWA_FILL, body, 0)

    lane = lax.broadcasted_iota(jnp.int32, (BLOCK, LANES), 1)
    first = lane < HEAD_DIM
    for r in range(nblk):
        n = i * nblk + r
        w0 = pl.multiple_of(n * BLOCK, BLOCK)
        tsel = jnp.where(n == 0, SWA_KV_HEADS, 0)
        rs = slice(r * BLOCK, (r + 1) * BLOCK)
        for g in range(SWA_KV_HEADS):
            slabs = [q_ref[rs, (2 * g + a) * LANES:(2 * g + a + 1) * LANES] for a in range(2)]
            zero = jnp.zeros_like(slabs[0])
            lhs = jnp.concatenate(
                [jnp.where(first, slabs[0], zero), jnp.where(first, zero, slabs[0]),
                 jnp.where(first, slabs[1], zero), jnp.where(first, zero, slabs[1])], axis=0)
            s = _dot_nt(lhs, kbuf_ref[g, pl.ds(w0, win), :])
            p = jnp.exp2(s - tab_ref[tsel + g]).astype(BF16)
            out = _dot(p, vbuf_ref[g, pl.ds(w0, win), :])
            for a in range(2):
                halves = []
                for k in range(2):
                    hq = SWA_GROUP * g + 2 * a + k
                    blk = out[(2 * a + k) * BLOCK:(2 * a + k + 1) * BLOCK]
                    denom = blk[:, LANES:2 * LANES] + scal_ref[3 * SWA_Q_HEADS + hq]
                    halves.append(blk[:, 0:LANES] / denom)
                o_ref[rs, (2 * g + a) * LANES:(2 * g + a + 1) * LANES] = (
                    jnp.where(first, halves[0], halves[1]).astype(BF16))


def _swa_online_kernel(scal_ref, q_ref, k_ref, v_ref, o_ref, *, tq):
    i = pl.program_id(1)
    qi = lax.broadcasted_iota(jnp.int32, (BLOCK, BLOCK), 0)
    kj = lax.broadcasted_iota(jnp.int32, (BLOCK, BLOCK), 1)
    lower = kj <= qi
    dist = jnp.where(lower, qi - kj, BLOCK + qi - kj).astype(F32)
    for r in range(tq // BLOCK):
        n = i * (tq // BLOCK) + r
        cur = pl.multiple_of(n * BLOCK, BLOCK)
        prev = pl.multiple_of(jnp.maximum(n - 1, 0) * BLOCK, BLOCK)
        valid = jnp.logical_or(lower, n > 0)
        k_cur = k_ref[pl.ds(cur, BLOCK), :]
        k_prev = k_ref[pl.ds(prev, BLOCK), :]
        v_cur = v_ref[pl.ds(cur, BLOCK), :]
        v_prev = v_ref[pl.ds(prev, BLOCK), :]
        outs = []
        for hq in range(SWA_Q_HEADS):
            g = hq // SWA_GROUP
            gs = slice(g * HEAD_DIM, (g + 1) * HEAD_DIM)
            qh = q_ref[r * BLOCK:(r + 1) * BLOCK, hq * HEAD_DIM:(hq + 1) * HEAD_DIM]
            s_cur = _dot_nt(qh, k_cur[:, gs])
            s_prev = _dot_nt(qh, k_prev[:, gs])
            s = jnp.where(lower, s_cur, s_prev) - scal_ref[hq] * dist
            s = jnp.where(valid, s, NEG)
            sink = scal_ref[SWA_Q_HEADS + hq]
            m = jnp.maximum(jnp.max(s, axis=-1, keepdims=True), sink)
            e = jnp.exp2(s - m)
            denom = jnp.sum(e, axis=-1, keepdims=True) + jnp.exp2(sink - m)
            p = e / denom
            p_cur = jnp.where(lower, p, 0.0).astype(BF16)
            p_prev = jnp.where(lower, 0.0, p).astype(BF16)
            outs.append(_dot(p_cur, v_cur[:, gs]) + _dot(p_prev, v_prev[:, gs]))
        o_ref[r * BLOCK:(r + 1) * BLOCK, :] = jnp.concatenate(outs, axis=-1).astype(BF16)


def _swa(qa, ka, va, scal, *, bsz, seq, tq, fixed_reference):
    assert seq % tq == 0 and tq % BLOCK == 0 and seq % SWA_FILL == 0
    nq = seq // tq
    nbytes = 2 * (2 * tq * BRANCH_WIDTH * 2 + 2 * seq * LANES * 2)
    if fixed_reference:
        kernel = functools.partial(_swa_fixed_kernel, tq=tq, seq=seq)
        rows = SWA_GROUP * BLOCK
        scratch = [
            pltpu.VMEM((SWA_KV_HEADS, seq + BLOCK, LANES), BF16),
            pltpu.VMEM((SWA_KV_HEADS, seq + BLOCK, 2 * LANES), BF16),
            pltpu.VMEM((2 * SWA_KV_HEADS, rows, 2 * BLOCK), F32),
        ]
        nbytes += (SWA_KV_HEADS * (seq + BLOCK) * 3 * LANES * 2 + 2 * SWA_KV_HEADS * rows * 2 * BLOCK * 4
                   + 8 * rows * 2 * BLOCK * 4)
        name = "swa"
    else:
        kernel = functools.partial(_swa_online_kernel, tq=tq)
        scratch = []
        name = "swa_online"
    return pl.pallas_call(
        kernel,
        grid=(bsz, nq),
        in_specs=[
            pl.BlockSpec(memory_space=pltpu.SMEM),
            pl.BlockSpec((tq, BRANCH_WIDTH), lambda b, i: (b * nq + i, 0)),
            pl.BlockSpec((seq, SWA_KV_WIDTH), lambda b, i: (b, 0)),
            pl.BlockSpec((seq, SWA_KV_WIDTH), lambda b, i: (b, 0)),
        ],
        out_specs=pl.BlockSpec((tq, BRANCH_WIDTH), lambda b, i: (b * nq + i, 0)),
        out_shape=jax.ShapeDtypeStruct(qa.shape, BF16),
        scratch_shapes=scratch,
        compiler_params=pltpu.CompilerParams(
            dimension_semantics=("arbitrary", "arbitrary"), vmem_limit_bytes=_vmem_limit(nbytes)),
        name=name,
    )(scal, qa, ka, va)


def _stack_q(q, tq):
    lane = lax.broadcasted_iota(jnp.int32, (tq, LANES), 1)
    zero = jnp.zeros_like(q)
    return jnp.where(lane < HEAD_DIM, q, zero), jnp.where(lane < HEAD_DIM, zero, q)


def _diff_finish(o1, o2, lam_init, lp_ref, g_ref):
    lp = lp_ref[...]
    lam = (jnp.exp(jnp.sum(lp[0:1] * lp[1:2], axis=-1, keepdims=True))
           - jnp.exp(jnp.sum(lp[2:3] * lp[3:4], axis=-1, keepdims=True)) + lam_init)
    o = o1 - lam * o2
    ms = jnp.mean(o * o, axis=-1, keepdims=True)
    return (o * lax.rsqrt(ms + EPS) * g_ref[...] * (1.0 - lam_init)).astype(BF16)


def _causal_keep(i, tq, start, width):
    col = start + lax.broadcasted_iota(jnp.int32, (1, width), 1)
    row = i * tq + lax.rem(lax.broadcasted_iota(jnp.int32, (2 * tq, 1), 0), tq)
    return col <= row


AUG_K_BASE = 3
AUG_Q_BASE = 9
DIFF_PART_SHIFT = 9
DIFF_PART = 1 << DIFF_PART_SHIFT
DIFF_KEY_BLOCK = 1024


def _split3(val):
    hi = val.astype(BF16).astype(F32)
    r1 = val - hi
    mid = r1.astype(BF16).astype(F32)
    lo = (r1 - mid).astype(BF16).astype(F32)
    return hi, mid, lo


def _aug_templates(slope2, rows, tmpl_ref):
    r = lax.broadcasted_iota(jnp.int32, (rows, LANES), 0).astype(F32)
    hi, mid, lo = _split3(slope2 * r)
    lane = lax.broadcasted_iota(jnp.int32, (rows, LANES), 1)
    parts = jnp.where(jnp.logical_or(lane == 0, lane == 6), hi,
                      jnp.where(jnp.logical_or(lane == 1, lane == 7), mid, lo))
    k_ones = jnp.logical_and(lane >= 6, lane < 12)
    tmpl_ref[0] = jnp.where(lane < 3, parts, jnp.where(k_ones, 1.0, 0.0))
    tmpl_ref[1] = jnp.where(lane < 6, 1.0, jnp.where(lane < 9, -parts, 0.0))


def _with_scalar_lanes(template, value, base):
    hi, mid, lo = _split3(jnp.full((1, LANES), value, F32))
    lane = lax.broadcasted_iota(jnp.int32, (1, LANES), 1)
    terms = jnp.where(lane == base, hi, jnp.where(lane == base + 1, mid, lo))
    here = jnp.logical_and(lane >= base, lane < base + 3)
    return jnp.where(here, terms, template).astype(BF16)


def _diff_fixed_kernel(scal_ref, q_ref, k_ref, v_ref, lp_ref, g_ref, o_ref,
                       kaug_ref, vaug_ref, qs_ref, acc_ref, tmpl_ref, *, tq, seq):
    h = pl.program_id(1)
    i = pl.program_id(2)
    lam_init = scal_ref[0]
    bound2 = scal_ref[1]
    slope2 = scal_ref[2 + h]
    part = DIFF_PART
    nparts = tq // part

    @pl.when(i == 0)
    def _build_kv():
        _aug_templates(slope2, part, tmpl_ref)

        def body(c, carry):
            r0 = pl.multiple_of(c * part, part)
            kaug_ref[pl.ds(r0, part), 0:LANES] = k_ref[pl.ds(r0, part), :]
            kaug_ref[pl.ds(r0, part), LANES:2 * LANES] = _with_scalar_lanes(
                tmpl_ref[0], slope2 * r0.astype(F32), AUG_K_BASE)
            vaug_ref[pl.ds(r0, part), 0:LANES] = v_ref[pl.ds(r0, part), :]
            vaug_ref[pl.ds(r0, part), LANES:2 * LANES] = jnp.ones((part, LANES), BF16)
            return carry
        lax.fori_loop(0, seq // part, body, 0)

    for c in range(nparts):
        q1, q2 = _stack_q(q_ref[c * part:(c + 1) * part, :], part)
        row0 = (i * tq + c * part).astype(F32)
        aug = _with_scalar_lanes(tmpl_ref[1], -(slope2 * row0 + bound2), AUG_Q_BASE)
        base = 2 * part * c
        qs_ref[base:base + part, 0:LANES] = q1
        qs_ref[base + part:base + 2 * part, 0:LANES] = q2
        qs_ref[base:base + part, LANES:2 * LANES] = aug
        qs_ref[base + part:base + 2 * part, LANES:2 * LANES] = aug
    acc_ref[...] = jnp.zeros(acc_ref.shape, F32)

    def full_block(j):
        start = pl.multiple_of(j * DIFF_KEY_BLOCK, DIFF_KEY_BLOCK)
        p = jnp.exp2(_dot_nt(qs_ref[...], kaug_ref[pl.ds(start, DIFF_KEY_BLOCK), :]))
        acc_ref[...] += _dot(p.astype(BF16), vaug_ref[pl.ds(start, DIFF_KEY_BLOCK), :])

    def block_pair(j, carry):
        full_block(2 * j)
        full_block(2 * j + 1)
        return carry

    lax.fori_loop(0, i * (tq // (2 * DIFF_KEY_BLOCK)), block_pair, 0)

    col = lax.broadcasted_iota(jnp.int32, (1, part), 1)
    for c in range(nparts):
        start = pl.multiple_of(i * tq + c * part, part)
        lo = 2 * part * c
        srow = lax.broadcasted_iota(jnp.int32, (2 * tq - lo, 1), 0)
        row = (c + lax.shift_right_logical(srow, DIFF_PART_SHIFT + 1)) * part + jnp.bitwise_and(srow, part - 1)
        p = jnp.exp2(_dot_nt(qs_ref[lo:2 * tq, :], kaug_ref[pl.ds(start, part), :]))
        p = jnp.where(col + c * part <= row, p, 0.0).astype(BF16)
        acc_ref[lo:2 * tq, :] += _dot(p, vaug_ref[pl.ds(start, part), :])

    for c in range(nparts):
        base = 2 * part * c
        a1 = acc_ref[base:base + part, :]
        a2 = acc_ref[base + part:base + 2 * part, :]
        o_ref[c * part:(c + 1) * part, :] = _diff_finish(
            a1[:, 0:LANES] / a1[:, LANES:2 * LANES], a2[:, 0:LANES] / a2[:, LANES:2 * LANES],
            lam_init, lp_ref, g_ref)


def _diff_online_kernel(scal_ref, q_ref, k_ref, v_ref, lp_ref, g_ref, o_ref,
                        qs_ref, m_ref, l_ref, acc_ref, *, tq):
    h = pl.program_id(1)
    i = pl.program_id(2)
    lam_init = scal_ref[0]
    slope2 = scal_ref[2 + h]

    q1, q2 = _stack_q(q_ref[...], tq)
    qs_ref[0:tq, :] = q1
    qs_ref[tq:2 * tq, :] = q2
    m_ref[...] = jnp.full(m_ref.shape, NEG, F32)
    l_ref[...] = jnp.zeros(l_ref.shape, F32)
    acc_ref[...] = jnp.zeros(acc_ref.shape, F32)

    def step(j, masked):
        start = pl.multiple_of(j * tq, tq)
        s = _dot_nt(qs_ref[...], k_ref[pl.ds(start, tq), :])
        col = start + lax.broadcasted_iota(jnp.int32, (1, tq), 1)
        s = s + slope2 * col.astype(F32)
        if masked:
            s = jnp.where(_causal_keep(i, tq, start, tq), s, NEG)
        m_prev = m_ref[...]
        m_new = jnp.maximum(m_prev, jnp.max(s, axis=-1, keepdims=True))
        alpha = jnp.exp2(m_prev - m_new)
        p = jnp.exp2(s - m_new)
        l_ref[...] = alpha * l_ref[...] + jnp.sum(p, axis=-1, keepdims=True)
        acc_ref[...] = alpha * acc_ref[...] + _dot(p.astype(BF16), v_ref[pl.ds(start, tq), :])
        m_ref[...] = m_new

    def body(j, carry):
        step(j, False)
        return carry

    lax.fori_loop(0, i, body, 0)
    step(i, True)

    acc = acc_ref[...]
    l = l_ref[...]
    o_ref[...] = _diff_finish(acc[0:tq] / l[0:tq], acc[tq:2 * tq] / l[tq:2 * tq],
                              lam_init, lp_ref, g_ref)


def _diff(qd, kd, vd, scal, lp, gsub, *, layer, bsz, seq, tq, fixed_reference):
    nq = seq // tq
    t = bsz * seq
    if fixed_reference:
        assert tq % (2 * DIFF_KEY_BLOCK) == 0 and seq % tq == 0
        kernel = functools.partial(_diff_fixed_kernel, tq=tq, seq=seq)
        scratch = [
            pltpu.VMEM((seq, 2 * LANES), BF16),
            pltpu.VMEM((seq, 2 * LANES), BF16),
            pltpu.VMEM((2 * tq, 2 * LANES), BF16),
            pltpu.VMEM((2 * tq, 2 * LANES), F32),
            pltpu.VMEM((2, DIFF_PART, LANES), F32),
        ]
        scratch_bytes = (2 * seq * 2 * LANES * 2 + 2 * tq * 2 * LANES * 6 + 2 * DIFF_PART * LANES * 4
                         + 2 * tq * DIFF_KEY_BLOCK * 6)
        name = "diff"
    else:
        assert seq % tq == 0
        kernel = functools.partial(_diff_online_kernel, tq=tq)
        scratch = [
            pltpu.VMEM((2 * tq, LANES), BF16),
            pltpu.VMEM((2 * tq, 1), F32),
            pltpu.VMEM((2 * tq, 1), F32),
            pltpu.VMEM((2 * tq, LANES), F32),
        ]
        scratch_bytes = 2 * tq * (LANES * 6 + 8) + 6 * 2 * tq * tq * 4
        name = "diff_online"
    nbytes = 2 * (2 * seq * LANES * 2 + 2 * tq * LANES * 2) + scratch_bytes
    return pl.pallas_call(
        kernel,
        grid=(bsz, DIFF_HEADS, nq),
        in_specs=[
            pl.BlockSpec(memory_space=pltpu.SMEM),
            pl.BlockSpec((None, tq, LANES), lambda b, h, i: (h, b * nq + i, 0)),
            pl.BlockSpec((None, seq, LANES), lambda b, h, i: (h, b, 0)),
            pl.BlockSpec((None, seq, LANES), lambda b, h, i: (h, b, 0)),
            pl.BlockSpec((None, 4, HEAD_DIM), lambda b, h, i: (layer, 0, 0)),
            pl.BlockSpec((None, 1, DIFF_V_DIM), lambda b, h, i: (layer, 0, 0)),
        ],
        out_specs=pl.BlockSpec((tq, LANES), lambda b, h, i: (b * nq + i, h)),
        out_shape=jax.ShapeDtypeStruct((t, DIFF_HEADS * DIFF_V_DIM), BF16),
        scratch_shapes=scratch,
        compiler_params=pltpu.CompilerParams(
            dimension_semantics=("arbitrary", "arbitrary", "arbitrary"),
            vmem_limit_bytes=_vmem_limit(nbytes)),
        name=name,
    )(scal, qd, kd, vd, lp, gsub)


def _mixffn_kernel(x_ref, oa_ref, ob_ref, gate_ref, wb_ref, wo_ref, gn_ref, win_ref, wout_ref,
                   out_ref, h_ref, *, chunk):
    d = x_ref.shape[1]
    up_a = _dot(oa_ref[...], wb_ref[0])
    up_b = _dot(ob_ref[...], wb_ref[1])
    merged = gate_ref[:, :d].astype(F32) * up_a + gate_ref[:, d:].astype(F32) * up_b
    x = x_ref[...] + _dot(merged.astype(BF16), wo_ref[...])
    ms = jnp.mean(x * x, axis=-1, keepdims=True)
    h_ref[...] = (x * lax.rsqrt(ms + EPS) * gn_ref[...]).astype(BF16)
    hidden = wout_ref.shape[0]
    acc = x
    for c in range(hidden // chunk):
        g = _dot(h_ref[...], win_ref[:, c * chunk:(c + 1) * chunk])
        u = _dot(h_ref[...], win_ref[:, hidden + c * chunk:hidden + (c + 1) * chunk])
        a = (g * jax.nn.sigmoid(g) * u).astype(BF16)
        acc = acc + _dot(a, wout_ref[c * chunk:(c + 1) * chunk, :])
    out_ref[...] = acc


def _mixffn(x, oa, ob, gates, wb, wo, gn, win, wout, *, layer, tm, chunk):
    t, d = x.shape
    hidden = wout.shape[1]
    row = lambda i: (i, 0)
    lconst = lambda i: (layer, 0, 0)
    resident = dict(pipeline_mode=pl.Buffered(1))
    assert t % tm == 0 and hidden % chunk == 0
    kernel = functools.partial(_mixffn_kernel, chunk=chunk)
    weight_bytes = (2 * BRANCH_WIDTH * d + d * d + 3 * d * hidden) * 2
    nbytes = (weight_bytes + 2 * (2 * tm * d * 4 + 2 * tm * BRANCH_WIDTH * 2 + tm * 2 * d * 2)
              + tm * d * 10)
    return pl.pallas_call(
        kernel,
        grid=(t // tm,),
        in_specs=[
            pl.BlockSpec((tm, d), row),
            pl.BlockSpec((tm, BRANCH_WIDTH), row),
            pl.BlockSpec((tm, BRANCH_WIDTH), row),
            pl.BlockSpec((tm, 2 * d), row),
            pl.BlockSpec((None,) + wb.shape[1:], lambda i: (layer, 0, 0, 0), **resident),
            pl.BlockSpec((None, d, d), lconst, **resident),
            pl.BlockSpec((None, 1, d), lconst),
            pl.BlockSpec((None, d, 2 * hidden), lconst, **resident),
            pl.BlockSpec((None, hidden, d), lconst, **resident),
        ],
        out_specs=pl.BlockSpec((tm, d), row),
        out_shape=jax.ShapeDtypeStruct(x.shape, F32),
        scratch_shapes=[pltpu.VMEM((tm, d), BF16)],
        compiler_params=pltpu.CompilerParams(
            dimension_semantics=("arbitrary",), vmem_limit_bytes=_vmem_limit(nbytes)),
        name="mixffn",
    )(x, oa, ob, gates, wb, wo, gn, win, wout)


def _alibi_slopes():
    return jnp.exp2(-8.0 * jnp.arange(1, N_ALIBI_HEADS + 1, dtype=F32) / N_ALIBI_HEADS)


def _group_mean_matrix():
    idx = jnp.arange(256) // HEAD_DIM
    return jnp.where(idx[:, None] == idx[None, :], 1.0 / HEAD_DIM, 0.0).astype(BF16)


def kernel(x, w_in, b_gate, w_branch, w_o, norm_mix, norm_ffn, qk_norm_swa, qk_norm_diff,
           attn_sinks, diff_lambda, diff_subln, w_ffn_in, w_ffn_out):
    bsz, seq, d = x.shape
    depth = w_in.shape[0]
    t = bsz * seq
    slopes = _alibi_slopes()
    slopes_swa, slopes_diff = slopes[:SWA_Q_HEADS], slopes[SWA_Q_HEADS:]
    bd = _group_mean_matrix()

    w_in_b, w_branch_b, w_o_b = w_in.astype(BF16), w_branch.astype(BF16), w_o.astype(BF16)
    w_ffn_in_b, w_ffn_out_b = w_ffn_in.astype(BF16), w_ffn_out.astype(BF16)
    gmix = norm_mix.astype(F32).reshape(depth, 1, d)
    gffn = norm_ffn.astype(F32).reshape(depth, 1, d)
    bg = b_gate.astype(F32).reshape(depth, 1, 2 * d)
    lp = diff_lambda.astype(F32)
    gsub = diff_subln.astype(F32).reshape(depth, 1, DIFF_V_DIM)
    q_fold = QK_SCALE * LOG2E
    gcol = jnp.concatenate(
        [jnp.tile(qk_norm_swa[:, 0], (1, SWA_Q_HEADS)) * q_fold,
         jnp.tile(qk_norm_swa[:, 1], (1, SWA_KV_HEADS)),
         jnp.ones((depth, SWA_KV_HEADS * HEAD_DIM), F32),
         jnp.tile(qk_norm_diff[:, 0], (1, 2 * DIFF_HEADS)) * q_fold,
         jnp.tile(qk_norm_diff[:, 1], (1, 2 * DIFF_HEADS))], axis=1).astype(F32).reshape(depth, 1, -1)

    def score_bound(qk_norm):
        gains = jnp.max(jnp.abs(qk_norm.astype(F32)), axis=-1)
        return SCORE_BOUND_PER_GAIN * gains[:, 0] * gains[:, 1]

    bound_a, bound_d = score_bound(qk_norm_swa), score_bound(qk_norm_diff)
    sinks2 = attn_sinks.astype(F32) * LOG2E
    ref_a = jnp.maximum(bound_a[:, None] * LOG2E, sinks2)
    scal_a = jnp.concatenate(
        [jnp.broadcast_to(slopes_swa * LOG2E, (depth, SWA_Q_HEADS)), sinks2, ref_a,
         jnp.exp2(sinks2 - ref_a)], axis=1)
    lam_init = jnp.array([0.8 - 0.6 * math.exp(-0.3 * l) for l in range(depth)], F32)
    scal_d = jnp.concatenate(
        [lam_init[:, None], bound_d[:, None] * LOG2E,
         jnp.broadcast_to(slopes_diff * LOG2E, (depth, DIFF_HEADS))], axis=1)

    xf = x.reshape(t, d)
    for l in range(depth):
        qa, ka, va, qd, kd, vd, gates = _proj(xf, gmix, w_in_b, gcol, bd, bg, layer=l, tm=PROJ_ROWS)

        o_a = lax.cond(
            bound_a[l] <= FIXED_REF_MAX_BOUND,
            lambda a: _swa(*a, bsz=bsz, seq=seq, tq=SWA_QUERIES, fixed_reference=True),
            lambda a: _swa(*a, bsz=bsz, seq=seq, tq=SWA_ONLINE_QUERIES, fixed_reference=False),
            (qa, ka, va, scal_a[l]))

        o_b = lax.cond(
            bound_d[l] <= FIXED_REF_MAX_BOUND,
            lambda a: _diff(*a, layer=l, bsz=bsz, seq=seq, tq=DIFF_QUERIES, fixed_reference=True),
            lambda a: _diff(*a, layer=l, bsz=bsz, seq=seq, tq=DIFF_ONLINE_QUERIES,
                            fixed_reference=False),
            (qd, kd, vd, scal_d[l], lp, gsub))

        xf = _mixffn(xf, o_a, o_b, gates, w_branch_b, w_o_b, gffn, w_ffn_in_b, w_ffn_out_b,
                     layer=l, tm=MIXFFN_ROWS, chunk=FFN_CHUNK)
    return xf.reshape(bsz, seq, d)
```

```python
import functools
import math

import jax
import jax.numpy as jnp
from jax import lax
from jax.experimental import pallas as pl
from jax.experimental.pallas import tpu as pltpu

F32 = jnp.float32
BF16 = jnp.bfloat16

HEAD_DIM = 64
BLOCK = 128
SWA_Q_HEADS = 8
SWA_KV_HEADS = 2
SWA_GROUP = SWA_Q_HEADS // SWA_KV_HEADS
DIFF_HEADS = 4
DIFF_V_DIM = 2 * HEAD_DIM
N_ALIBI_HEADS = SWA_Q_HEADS + DIFF_HEADS
NEG = -1e30
EPS = 1e-6
QK_SCALE = HEAD_DIM ** -0.5
LOG2E = math.log2(math.e)
SCORE_BOUND_PER_GAIN = HEAD_DIM * QK_SCALE
FIXED_REF_MAX_BOUND = 20.0

V7X_VMEM_BYTES = 64 * 1024 * 1024
LANES = 128
BRANCH_WIDTH = SWA_Q_HEADS * HEAD_DIM
SWA_KV_WIDTH = SWA_KV_HEADS * HEAD_DIM

PROJ_ROWS = 1024
MIXFFN_ROWS = 1024
FFN_CHUNK = 256
SWA_QUERIES = 2048
SWA_ONLINE_QUERIES = 512
DIFF_QUERIES = 2048
DIFF_ONLINE_QUERIES = 256


def _vmem_limit(nbytes):
    return int(min(max(2 * nbytes, 32 * 1024 * 1024), V7X_VMEM_BYTES - 4 * 1024 * 1024))


def _dot(a, b):
    return jnp.dot(a, b, preferred_element_type=F32)


def _dot_nt(a, b):
    return lax.dot_general(a, b, (((1,), (1,)), ((), ())), preferred_element_type=F32)


PROJ_CHUNK = 256
PROJ_NORMED = 1792


def _proj_kernel(x_ref, gmix_ref, w_ref, gcol_ref, bd_ref, bg_ref,
                 qa_ref, ka_ref, va_ref, qd_ref, kd_ref, vd_ref, gate_ref, h_ref, y_ref):
    tm = x_ref.shape[0]
    x = x_ref[...]
    ms = jnp.mean(x * x, axis=-1, keepdims=True)
    h_ref[...] = (x * lax.rsqrt(ms + EPS) * gmix_ref[...]).astype(BF16)

    def proj(c0, width):
        return _dot(h_ref[...], w_ref[:, c0:c0 + width])

    nchunk = PROJ_NORMED // PROJ_CHUNK
    y_ref[...] = proj(0, PROJ_NORMED)
    sq = jnp.concatenate(
        [jnp.square(y_ref[:, PROJ_CHUNK * c:PROJ_CHUNK * (c + 1)]).astype(BF16)
         for c in range(nchunk)], axis=0)
    inv = lax.rsqrt(_dot(sq, bd_ref[...]) + EPS)

    def normed(c):
        cs = slice(PROJ_CHUNK * c, PROJ_CHUNK * (c + 1))
        return (y_ref[:, cs] * inv[c * tm:(c + 1) * tm] * gcol_ref[:, cs]).astype(BF16)

    for c in range(2):
        qa_ref[:, PROJ_CHUNK * c:PROJ_CHUNK * (c + 1)] = normed(c)
    ka_ref[...] = normed(2)[:, :LANES]
    va_ref[...] = y_ref[:, 2 * PROJ_CHUNK + LANES:3 * PROJ_CHUNK].astype(BF16)
    for c in range(2):
        y = normed(3 + c)
        qd_ref[2 * c] = y[:, :LANES]
        qd_ref[2 * c + 1] = y[:, LANES:]
    for c in range(2):
        y = normed(5 + c)
        kd_ref[2 * c] = y[:, :LANES]
        kd_ref[2 * c + 1] = y[:, LANES:]
    y = proj(PROJ_NORMED, DIFF_HEADS * LANES).astype(BF16)
    for hd in range(DIFF_HEADS):
        vd_ref[hd] = y[:, hd * LANES:(hd + 1) * LANES]
    g0 = PROJ_NORMED + DIFF_HEADS * LANES
    gate_ref[...] = jax.nn.sigmoid(proj(g0, gate_ref.shape[1]) + bg_ref[...]).astype(BF16)


def _proj(x, gmix, w, gcol, bd, bg, *, layer, tm):
    t, d = x.shape
    ncols = w.shape[2]
    assert t % tm == 0
    assert ncols == BRANCH_WIDTH + 2 * SWA_KV_WIDTH + 3 * DIFF_HEADS * DIFF_V_DIM + 2 * d
    assert PROJ_NORMED == BRANCH_WIDTH + 2 * SWA_KV_WIDTH + 2 * DIFF_HEADS * DIFF_V_DIM
    grid = (t // tm,)
    const = lambda i: (0, 0)
    out_shape = (
        jax.ShapeDtypeStruct((t, BRANCH_WIDTH), BF16),
        jax.ShapeDtypeStruct((t, SWA_KV_WIDTH), BF16),
        jax.ShapeDtypeStruct((t, SWA_KV_WIDTH), BF16),
        jax.ShapeDtypeStruct((DIFF_HEADS, t, DIFF_V_DIM), BF16),
        jax.ShapeDtypeStruct((DIFF_HEADS, t, DIFF_V_DIM), BF16),
        jax.ShapeDtypeStruct((DIFF_HEADS, t, DIFF_V_DIM), BF16),
        jax.ShapeDtypeStruct((t, 2 * d), BF16),
    )
    row = lambda i: (i, 0)
    hrow = lambda i: (0, i, 0)
    out_specs = (
        pl.BlockSpec((tm, BRANCH_WIDTH), row),
        pl.BlockSpec((tm, SWA_KV_WIDTH), row),
        pl.BlockSpec((tm, SWA_KV_WIDTH), row),
        pl.BlockSpec((DIFF_HEADS, tm, DIFF_V_DIM), hrow),
        pl.BlockSpec((DIFF_HEADS, tm, DIFF_V_DIM), hrow),
        pl.BlockSpec((DIFF_HEADS, tm, DIFF_V_DIM), hrow),
        pl.BlockSpec((tm, 2 * d), row),
    )
    lconst = lambda i: (layer, 0, 0)
    in_specs = [
        pl.BlockSpec((tm, d), row),
        pl.BlockSpec((None, 1, d), lconst),
        pl.BlockSpec((None, d, ncols), lconst, pipeline_mode=pl.Buffered(1)),
        pl.BlockSpec((None, 1, gcol.shape[2]), lconst),
        pl.BlockSpec(bd.shape, const),
        pl.BlockSpec((None, 1, 2 * d), lconst),
    ]
    nbytes = (2 * tm * d * 4 + d * ncols * 2 + 2 * tm * ncols * 2 + tm * d * 2
              + 3 * tm * PROJ_NORMED * 4)
    return pl.pallas_call(
        _proj_kernel,
        grid=grid,
        in_specs=in_specs,
        out_specs=out_specs,
        out_shape=out_shape,
        scratch_shapes=[pltpu.VMEM((tm, d), BF16), pltpu.VMEM((tm, PROJ_NORMED), F32)],
        compiler_params=pltpu.CompilerParams(
            dimension_semantics=("arbitrary",), vmem_limit_bytes=_vmem_limit(nbytes)),
        name="proj",
    )(x, gmix, w, gcol, bd, bg)


SWA_FILL = 512
MASKED_LOGIT_SHIFT = 1e30


def _swa_fixed_kernel(scal_ref, q_ref, k_ref, v_ref, o_ref, kbuf_ref, vbuf_ref, tab_ref,
                      *, tq, seq):
    b = pl.program_id(0)
    i = pl.program_id(1)
    nblk = tq // BLOCK
    rows = SWA_GROUP * BLOCK
    win = 2 * BLOCK

    @pl.when(jnp.logical_and(b == 0, i == 0))
    def _build_tables():
        row = lax.broadcasted_iota(jnp.int32, (rows, win), 0)
        col = lax.broadcasted_iota(jnp.int32, (rows, win), 1)
        head = lax.shift_right_logical(row, 7)
        dist = jnp.bitwise_and(row, BLOCK - 1) + BLOCK - col
        valid = jnp.logical_and(dist >= 0, dist < BLOCK)
        has_key = jnp.logical_and(valid, col >= BLOCK)
        distf = dist.astype(F32)
        for g in range(SWA_KV_HEADS):
            def per_head(base):
                h0 = SWA_GROUP * g
                return jnp.where(head == 0, scal_ref[base + h0],
                                 jnp.where(head == 1, scal_ref[base + h0 + 1],
                                           jnp.where(head == 2, scal_ref[base + h0 + 2],
                                                     scal_ref[base + h0 + 3])))
            t = per_head(0) * distf + per_head(2 * SWA_Q_HEADS)
            tab_ref[g] = jnp.where(valid, t, MASKED_LOGIT_SHIFT)
            tab_ref[SWA_KV_HEADS + g] = jnp.where(has_key, t, MASKED_LOGIT_SHIFT)

    @pl.when(i == 0)
    def _stage_kv():
        kbuf_ref[:, 0:BLOCK, :] = jnp.zeros((SWA_KV_HEADS, BLOCK, LANES), BF16)
        vbuf_ref[:, 0:BLOCK, :] = jnp.zeros((SWA_KV_HEADS, BLOCK, 2 * LANES), BF16)
        lane = lax.broadcasted_iota(jnp.int32, (SWA_FILL, LANES), 1)
        first = lane < HEAD_DIM

        def body(c, carry):
            r0 = pl.multiple_of(c * SWA_FILL, SWA_FILL)
            dst = pl.ds(pl.multiple_of(r0 + BLOCK, BLOCK), SWA_FILL)
            for src_ref, dst_ref in ((k_ref, kbuf_ref), (v_ref, vbuf_ref)):
                x = src_ref[pl.ds(r0, SWA_FILL), :].astype(F32)
                xr = pltpu.roll(x, HEAD_DIM, 1)
                dst_ref[0, dst, 0:LANES] = jnp.where(first, x, xr).astype(BF16)
                dst_ref[1, dst, 0:LANES] = jnp.where(first, xr, x).astype(BF16)
            vbuf_ref[:, dst, LANES:2 * LANES] = jnp.ones((SWA_KV_HEADS, SWA_FILL, LANES), BF16)
            return carry
        lax.fori_loop(0, seq // SWA_FILL, body, 0)

    lane = lax.broadcasted_iota(jnp.int32, (BLOCK, LANES), 1)
    first = lane < HEAD_DIM
    for r in range(nblk):
        n = i * nblk + r
        w0 = pl.multiple_of(n * BLOCK, BLOCK)
        tsel = jnp.where(n == 0, SWA_KV_HEADS, 0)
        rs = slice(r * BLOCK, (r + 1) * BLOCK)
        for g in range(SWA_KV_HEADS):
            slabs = [q_ref[rs, (2 * g + a) * LANES:(2 * g + a + 1) * LANES] for a in range(2)]
            zero = jnp.zeros_like(slabs[0])
            lhs = jnp.concatenate(
                [jnp.where(first, slabs[0], zero), jnp.where(first, zero, slabs[0]),
                 jnp.where(first, slabs[1], zero), jnp.where(first, zero, slabs[1])], axis=0)
            s = _dot_nt(lhs, kbuf_ref[g, pl.ds(w0, win), :])
            p = jnp.exp2(s - tab_ref[tsel + g]).astype(BF16)
            out = _dot(p, vbuf_ref[g, pl.ds(w0, win), :])
            for a in range(2):
                halves = []
                for k in range(2):
                    hq = SWA_GROUP * g + 2 * a + k
                    blk = out[(2 * a + k) * BLOCK:(2 * a + k + 1) * BLOCK]
                    denom = blk[:, LANES:2 * LANES] + scal_ref[3 * SWA_Q_HEADS + hq]
                    halves.append(blk[:, 0:LANES] / denom)
                o_ref[rs, (2 * g + a) * LANES:(2 * g + a + 1) * LANES] = (
                    jnp.where(first, halves[0], halves[1]).astype(BF16))


def _swa_online_kernel(scal_ref, q_ref, k_ref, v_ref, o_ref, *, tq):
    i = pl.program_id(1)
    qi = lax.broadcasted_iota(jnp.int32, (BLOCK, BLOCK), 0)
    kj = lax.broadcasted_iota(jnp.int32, (BLOCK, BLOCK), 1)
    lower = kj <= qi
    dist = jnp.where(lower, qi - kj, BLOCK + qi - kj).astype(F32)
    for r in range(tq // BLOCK):
        n = i * (tq // BLOCK) + r
        cur = pl.multiple_of(n * BLOCK, BLOCK)
        prev = pl.multiple_of(jnp.maximum(n - 1, 0) * BLOCK, BLOCK)
        valid = jnp.logical_or(lower, n > 0)
        k_cur = k_ref[pl.ds(cur, BLOCK), :]
        k_prev = k_ref[pl.ds(prev, BLOCK), :]
        v_cur = v_ref[pl.ds(cur, BLOCK), :]
        v_prev = v_ref[pl.ds(prev, BLOCK), :]
        outs = []
        for hq in range(SWA_Q_HEADS):
            g = hq // SWA_GROUP
            gs = slice(g * HEAD_DIM, (g + 1) * HEAD_DIM)
            qh = q_ref[r * BLOCK:(r + 1) * BLOCK, hq * HEAD_DIM:(hq + 1) * HEAD_DIM]
            s_cur = _dot_nt(qh, k_cur[:, gs])
            s_prev = _dot_nt(qh, k_prev[:, gs])
            s = jnp.where(lower, s_cur, s_prev) - scal_ref[hq] * dist
            s = jnp.where(valid, s, NEG)
            sink = scal_ref[SWA_Q_HEADS + hq]
            m = jnp.maximum(jnp.max(s, axis=-1, keepdims=True), sink)
            e = jnp.exp2(s - m)
            denom = jnp.sum(e, axis=-1, keepdims=True) + jnp.exp2(sink - m)
            p = e / denom
            p_cur = jnp.where(lower, p, 0.0).astype(BF16)
            p_prev = jnp.where(lower, 0.0, p).astype(BF16)
            outs.append(_dot(p_cur, v_cur[:, gs]) + _dot(p_prev, v_prev[:, gs]))
        o_ref[r * BLOCK:(r + 1) * BLOCK, :] = jnp.concatenate(outs, axis=-1).astype(BF16)


def _swa(qa, ka, va, scal, *, bsz, seq, tq, fixed_reference):
    assert seq % tq == 0 and tq % BLOCK == 0 and seq % SWA_FILL == 0
    nq = seq // tq
    nbytes = 2 * (2 * tq * BRANCH_WIDTH * 2 + 2 * seq * LANES * 2)
    if fixed_reference:
        kernel = functools.partial(_swa_fixed_kernel, tq=tq, seq=seq)
        rows = SWA_GROUP * BLOCK
        scratch = [
            pltpu.VMEM((SWA_KV_HEADS, seq + BLOCK, LANES), BF16),
            pltpu.VMEM((SWA_KV_HEADS, seq + BLOCK, 2 * LANES), BF16),
            pltpu.VMEM((2 * SWA_KV_HEADS, rows, 2 * BLOCK), F32),
        ]
        nbytes += (SWA_KV_HEADS * (seq + BLOCK) * 3 * LANES * 2 + 2 * SWA_KV_HEADS * rows * 2 * BLOCK * 4
                   + 8 * rows * 2 * BLOCK * 4)
        name = "swa"
    else:
        kernel = functools.partial(_swa_online_kernel, tq=tq)
        scratch = []
        name = "swa_online"
    return pl.pallas_call(
        kernel,
        grid=(bsz, nq),
        in_specs=[
            pl.BlockSpec(memory_space=pltpu.SMEM),
            pl.BlockSpec((tq, BRANCH_WIDTH), lambda b, i: (b * nq + i, 0)),
            pl.BlockSpec((seq, SWA_KV_WIDTH), lambda b, i: (b, 0)),
            pl.BlockSpec((seq, SWA_KV_WIDTH), lambda b, i: (b, 0)),
        ],
        out_specs=pl.BlockSpec((tq, BRANCH_WIDTH), lambda b, i: (b * nq + i, 0)),
        out_shape=jax.ShapeDtypeStruct(qa.shape, BF16),
        scratch_shapes=scratch,
        compiler_params=pltpu.CompilerParams(
            dimension_semantics=("arbitrary", "arbitrary"), vmem_limit_bytes=_vmem_limit(nbytes)),
        name=name,
    )(scal, qa, ka, va)


def _stack_q(q, tq):
    lane = lax.broadcasted_iota(jnp.int32, (tq, LANES), 1)
    zero = jnp.zeros_like(q)
    return jnp.where(lane < HEAD_DIM, q, zero), jnp.where(lane < HEAD_DIM, zero, q)


def _diff_finish(o1, o2, lam_init, lp_ref, g_ref):
    lp = lp_ref[...]
    lam = (jnp.exp(jnp.sum(lp[0:1] * lp[1:2], axis=-1, keepdims=True))
           - jnp.exp(jnp.sum(lp[2:3] * lp[3:4], axis=-1, keepdims=True)) + lam_init)
    o = o1 - lam * o2
    ms = jnp.mean(o * o, axis=-1, keepdims=True)
    return (o * lax.rsqrt(ms + EPS) * g_ref[...] * (1.0 - lam_init)).astype(BF16)


def _causal_keep(i, tq, start, width):
    col = start + lax.broadcasted_iota(jnp.int32, (1, width), 1)
    row = i * tq + lax.rem(lax.broadcasted_iota(jnp.int32, (2 * tq, 1), 0), tq)
    return col <= row


AUG_K_BASE = 3
AUG_Q_BASE = 9
DIFF_PART_SHIFT = 9
DIFF_PART = 1 << DIFF_PART_SHIFT
DIFF_KEY_BLOCK = 1024


def _split3(val):
    hi = val.astype(BF16).astype(F32)
    r1 = val - hi
    mid = r1.astype(BF16).astype(F32)
    lo = (r1 - mid).astype(BF16).astype(F32)
    return hi, mid, lo


def _aug_templates(slope2, rows, tmpl_ref):
    r = lax.broadcasted_iota(jnp.int32, (rows, LANES), 0).astype(F32)
    hi, mid, lo = _split3(slope2 * r)
    lane = lax.broadcasted_iota(jnp.int32, (rows, LANES), 1)
    parts = jnp.where(jnp.logical_or(lane == 0, lane == 6), hi,
                      jnp.where(jnp.logical_or(lane == 1, lane == 7), mid, lo))
    k_ones = jnp.logical_and(lane >= 6, lane < 12)
    tmpl_ref[0] = jnp.where(lane < 3, parts, jnp.where(k_ones, 1.0, 0.0))
    tmpl_ref[1] = jnp.where(lane < 6, 1.0, jnp.where(lane < 9, -parts, 0.0))


def _with_scalar_lanes(template, value, base):
    hi, mid, lo = _split3(jnp.full((1, LANES), value, F32))
    lane = lax.broadcasted_iota(jnp.int32, (1, LANES), 1)
    terms = jnp.where(lane == base, hi, jnp.where(lane == base + 1, mid, lo))
    here = jnp.logical_and(lane >= base, lane < base + 3)
    return jnp.where(here, terms, template).astype(BF16)


def _diff_fixed_kernel(scal_ref, q_ref, k_ref, v_ref, lp_ref, g_ref, o_ref,
                       kaug_ref, vaug_ref, qs_ref, acc_ref, tmpl_ref, *, tq, seq):
    h = pl.program_id(1)
    i = pl.program_id(2)
    lam_init = scal_ref[0]
    bound2 = scal_ref[1]
    slope2 = scal_ref[2 + h]
    part = DIFF_PART
    nparts = tq // part

    @pl.when(i == 0)
    def _build_kv():
        _aug_templates(slope2, part, tmpl_ref)

        def body(c, carry):
            r0 = pl.multiple_of(c * part, part)
            kaug_ref[pl.ds(r0, part), 0:LANES] = k_ref[pl.ds(r0, part), :]
            kaug_ref[pl.ds(r0, part), LANES:2 * LANES] = _with_scalar_lanes(
                tmpl_ref[0], slope2 * r0.astype(F32), AUG_K_BASE)
            vaug_ref[pl.ds(r0, part), 0:LANES] = v_ref[pl.ds(r0, part), :]
            vaug_ref[pl.ds(r0, part), LANES:2 * LANES] = jnp.ones((part, LANES), BF16)
            return carry
        lax.fori_loop(0, seq // part, body, 0)

    for c in range(nparts):
        q1, q2 = _stack_q(q_ref[c * part:(c + 1) * part, :], part)
        row0 = (i * tq + c * part).astype(F32)
        aug = _with_scalar_lanes(tmpl_ref[1], -(slope2 * row0 + bound2), AUG_Q_BASE)
        base = 2 * part * c
        qs_ref[base:base + part, 0:LANES] = q1
        qs_ref[base + part:base + 2 * part, 0:LANES] = q2
        qs_ref[base:base + part, LANES:2 * LANES] = aug
        qs_ref[base + part:base + 2 * part, LANES:2 * LANES] = aug
    acc_ref[...] = jnp.zeros(acc_ref.shape, F32)

    def full_block(j):
        start = pl.multiple_of(j * DIFF_KEY_BLOCK, DIFF_KEY_BLOCK)
        p = jnp.exp2(_dot_nt(qs_ref[...], kaug_ref[pl.ds(start, DIFF_KEY_BLOCK), :]))
        acc_ref[...] += _dot(p.astype(BF16), vaug_ref[pl.ds(start, DIFF_KEY_BLOCK), :])

    def block_pair(j, carry):
        full_block(2 * j)
        full_block(2 * j + 1)
        return carry

    lax.fori_loop(0, i * (tq // (2 * DIFF_KEY_BLOCK)), block_pair, 0)

    col = lax.broadcasted_iota(jnp.int32, (1, part), 1)
    for c in range(nparts):
        start = pl.multiple_of(i * tq + c * part, part)
        lo = 2 * part * c
        srow = lax.broadcasted_iota(jnp.int32, (2 * tq - lo, 1), 0)
        row = (c + lax.shift_right_logical(srow, DIFF_PART_SHIFT + 1)) * part + jnp.bitwise_and(srow, part - 1)
        p = jnp.exp2(_dot_nt(qs_ref[lo:2 * tq, :], kaug_ref[pl.ds(start, part), :]))
        p = jnp.where(col + c * part <= row, p, 0.0).astype(BF16)
        acc_ref[lo:2 * tq, :] += _dot(p, vaug_ref[pl.ds(start, part), :])

    for c in range(nparts):
        base = 2 * part * c
        a1 = acc_ref[base:base + part, :]
        a2 = acc_ref[base + part:base + 2 * part, :]
        o_ref[c * part:(c + 1) * part, :] = _diff_finish(
            a1[:, 0:LANES] / a1[:, LANES:2 * LANES], a2[:, 0:LANES] / a2[:, LANES:2 * LANES],
            lam_init, lp_ref, g_ref)


def _diff_online_kernel(scal_ref, q_ref, k_ref, v_ref, lp_ref, g_ref, o_ref,
                        qs_ref, m_ref, l_ref, acc_ref, *, tq):
    h = pl.program_id(1)
    i = pl.program_id(2)
    lam_init = scal_ref[0]
    slope2 = scal_ref[2 + h]

    q1, q2 = _stack_q(q_ref[...], tq)
    qs_ref[0:tq, :] = q1
    qs_ref[tq:2 * tq, :] = q2
    m_ref[...] = jnp.full(m_ref.shape, NEG, F32)
    l_ref[...] = jnp.zeros(l_ref.shape, F32)
    acc_ref[...] = jnp.zeros(acc_ref.shape, F32)

    def step(j, masked):
        start = pl.multiple_of(j * tq, tq)
        s = _dot_nt(qs_ref[...], k_ref[pl.ds(start, tq), :])
        col = start + lax.broadcasted_iota(jnp.int32, (1, tq), 1)
        s = s + slope2 * col.astype(F32)
        if masked:
            s = jnp.where(_causal_keep(i, tq, start, tq), s, NEG)
        m_prev = m_ref[...]
        m_new = jnp.maximum(m_prev, jnp.max(s, axis=-1, keepdims=True))
        alpha = jnp.exp2(m_prev - m_new)
        p = jnp.exp2(s - m_new)
        l_ref[...] = alpha * l_ref[...] + jnp.sum(p, axis=-1, keepdims=True)
        acc_ref[...] = alpha * acc_ref[...] + _dot(p.astype(BF16), v_ref[pl.ds(start, tq), :])
        m_ref[...] = m_new

    def body(j, carry):
        step(j, False)
        return carry

    lax.fori_loop(0, i, body, 0)
    step(i, True)

    acc = acc_ref[...]
    l = l_ref[...]
    o_ref[...] = _diff_finish(acc[0:tq] / l[0:tq], acc[tq:2 * tq] / l[tq:2 * tq],
                              lam_init, lp_ref, g_ref)


def _diff(qd, kd, vd, scal, lp, gsub, *, layer, bsz, seq, tq, fixed_reference):
    nq = seq // tq
    t = bsz * seq
    if fixed_reference:
        assert tq % (2 * DIFF_KEY_BLOCK) == 0 and seq % tq == 0
        kernel = functools.partial(_diff_fixed_kernel, tq=tq, seq=seq)
        scratch = [
            pltpu.VMEM((seq, 2 * LANES), BF16),
            pltpu.VMEM((seq, 2 * LANES), BF16),
            pltpu.VMEM((2 * tq, 2 * LANES), BF16),
            pltpu.VMEM((2 * tq, 2 * LANES), F32),
            pltpu.VMEM((2, DIFF_PART, LANES), F32),
        ]
        scratch_bytes = (2 * seq * 2 * LANES * 2 + 2 * tq * 2 * LANES * 6 + 2 * DIFF_PART * LANES * 4
                         + 2 * tq * DIFF_KEY_BLOCK * 6)
        name = "diff"
    else:
        assert seq % tq == 0
        kernel = functools.partial(_diff_online_kernel, tq=tq)
        scratch = [
            pltpu.VMEM((2 * tq, LANES), BF16),
            pltpu.VMEM((2 * tq, 1), F32),
            pltpu.VMEM((2 * tq, 1), F32),
            pltpu.VMEM((2 * tq, LANES), F32),
        ]
        scratch_bytes = 2 * tq * (LANES * 6 + 8) + 6 * 2 * tq * tq * 4
        name = "diff_online"
    nbytes = 2 * (2 * seq * LANES * 2 + 2 * tq * LANES * 2) + scratch_bytes
    return pl.pallas_call(
        kernel,
        grid=(bsz, DIFF_HEADS, nq),
        in_specs=[
            pl.BlockSpec(memory_space=pltpu.SMEM),
            pl.BlockSpec((None, tq, LANES), lambda b, h, i: (h, b * nq + i, 0)),
            pl.BlockSpec((None, seq, LANES), lambda b, h, i: (h, b, 0)),
            pl.BlockSpec((None, seq, LANES), lambda b, h, i: (h, b, 0)),
            pl.BlockSpec((None, 4, HEAD_DIM), lambda b, h, i: (layer, 0, 0)),
            pl.BlockSpec((None, 1, DIFF_V_DIM), lambda b, h, i: (layer, 0, 0)),
        ],
        out_specs=pl.BlockSpec((tq, LANES), lambda b, h, i: (b * nq + i, h)),
        out_shape=jax.ShapeDtypeStruct((t, DIFF_HEADS * DIFF_V_DIM), BF16),
        scratch_shapes=scratch,
        compiler_params=pltpu.CompilerParams(
            dimension_semantics=("arbitrary", "arbitrary", "arbitrary"),
            vmem_limit_bytes=_vmem_limit(nbytes)),
        name=name,
    )(scal, qd, kd, vd, lp, gsub)


def _mixffn_kernel(x_ref, oa_ref, ob_ref, gate_ref, wb_ref, wo_ref, gn_ref, win_ref, wout_ref,
                   out_ref, h_ref, *, chunk):
    d = x_ref.shape[1]
    up_a = _dot(oa_ref[...], wb_ref[0])
    up_b = _dot(ob_ref[...], wb_ref[1])
    merged = gate_ref[:, :d].astype(F32) * up_a + gate_ref[:, d:].astype(F32) * up_b
    x = x_ref[...] + _dot(merged.astype(BF16), wo_ref[...])
    ms = jnp.mean(x * x, axis=-1, keepdims=True)
    h_ref[...] = (x * lax.rsqrt(ms + EPS) * gn_ref[...]).astype(BF16)
    hidden = wout_ref.shape[0]
    acc = x
    for c in range(hidden // chunk):
        g = _dot(h_ref[...], win_ref[:, c * chunk:(c + 1) * chunk])
        u = _dot(h_ref[...], win_ref[:, hidden + c * chunk:hidden + (c + 1) * chunk])
        a = (g * jax.nn.sigmoid(g) * u).astype(BF16)
        acc = acc + _dot(a, wout_ref[c * chunk:(c + 1) * chunk, :])
    out_ref[...] = acc


def _mixffn(x, oa, ob, gates, wb, wo, gn, win, wout, *, layer, tm, chunk):
    t, d = x.shape
    hidden = wout.shape[1]
    row = lambda i: (i, 0)
    lconst = lambda i: (layer, 0, 0)
    resident = dict(pipeline_mode=pl.Buffered(1))
    assert t % tm == 0 and hidden % chunk == 0
    kernel = functools.partial(_mixffn_kernel, chunk=chunk)
    weight_bytes = (2 * BRANCH_WIDTH * d + d * d + 3 * d * hidden) * 2
    nbytes = (weight_bytes + 2 * (2 * tm * d * 4 + 2 * tm * BRANCH_WIDTH * 2 + tm * 2 * d * 2)
              + tm * d * 10)
    return pl.pallas_call(
        kernel,
        grid=(t // tm,),
        in_specs=[
            pl.BlockSpec((tm, d), row),
            pl.BlockSpec((tm, BRANCH_WIDTH), row),
            pl.BlockSpec((tm, BRANCH_WIDTH), row),
            pl.BlockSpec((tm, 2 * d), row),
            pl.BlockSpec((None,) + wb.shape[1:], lambda i: (layer, 0, 0, 0), **resident),
            pl.BlockSpec((None, d, d), lconst, **resident),
            pl.BlockSpec((None, 1, d), lconst),
            pl.BlockSpec((None, d, 2 * hidden), lconst, **resident),
            pl.BlockSpec((None, hidden, d), lconst, **resident),
        ],
        out_specs=pl.BlockSpec((tm, d), row),
        out_shape=jax.ShapeDtypeStruct(x.shape, F32),
        scratch_shapes=[pltpu.VMEM((tm, d), BF16)],
        compiler_params=pltpu.CompilerParams(
            dimension_semantics=("arbitrary",), vmem_limit_bytes=_vmem_limit(nbytes)),
        name="mixffn",
    )(x, oa, ob, gates, wb, wo, gn, win, wout)


def _alibi_slopes():
    return jnp.exp2(-8.0 * jnp.arange(1, N_ALIBI_HEADS + 1, dtype=F32) / N_ALIBI_HEADS)


def _group_mean_matrix():
    idx = jnp.arange(256) // HEAD_DIM
    return jnp.where(idx[:, None] == idx[None, :], 1.0 / HEAD_DIM, 0.0).astype(BF16)


def kernel(x, w_in, b_gate, w_branch, w_o, norm_mix, norm_ffn, qk_norm_swa, qk_norm_diff,
           attn_sinks, diff_lambda, diff_subln, w_ffn_in, w_ffn_out):
    bsz, seq, d = x.shape
    depth = w_in.shape[0]
    t = bsz * seq
    slopes = _alibi_slopes()
    slopes_swa, slopes_diff = slopes[:SWA_Q_HEADS], slopes[SWA_Q_HEADS:]
    bd = _group_mean_matrix()

    w_in_b, w_branch_b, w_o_b = w_in.astype(BF16), w_branch.astype(BF16), w_o.astype(BF16)
    w_ffn_in_b, w_ffn_out_b = w_ffn_in.astype(BF16), w_ffn_out.astype(BF16)
    gmix = norm_mix.astype(F32).reshape(depth, 1, d)
    gffn = norm_ffn.astype(F32).reshape(depth, 1, d)
    bg = b_gate.astype(F32).reshape(depth, 1, 2 * d)
    lp = diff_lambda.astype(F32)
    gsub = diff_subln.astype(F32).reshape(depth, 1, DIFF_V_DIM)
    q_fold = QK_SCALE * LOG2E
    gcol = jnp.concatenate(
        [jnp.tile(qk_norm_swa[:, 0], (1, SWA_Q_HEADS)) * q_fold,
         jnp.tile(qk_norm_swa[:, 1], (1, SWA_KV_HEADS)),
         jnp.ones((depth, SWA_KV_HEADS * HEAD_DIM), F32),
         jnp.tile(qk_norm_diff[:, 0], (1, 2 * DIFF_HEADS)) * q_fold,
         jnp.tile(qk_norm_diff[:, 1], (1, 2 * DIFF_HEADS))], axis=1).astype(F32).reshape(depth, 1, -1)

    def score_bound(qk_norm):
        gains = jnp.max(jnp.abs(qk_norm.astype(F32)), axis=-1)
        return SCORE_BOUND_PER_GAIN * gains[:, 0] * gains[:, 1]

    bound_a, bound_d = score_bound(qk_norm_swa), score_bound(qk_norm_diff)
    sinks2 = attn_sinks.astype(F32) * LOG2E
    ref_a = jnp.maximum(bound_a[:, None] * LOG2E, sinks2)
    scal_a = jnp.concatenate(
        [jnp.broadcast_to(slopes_swa * LOG2E, (depth, SWA_Q_HEADS)), sinks2, ref_a,
         jnp.exp2(sinks2 - ref_a)], axis=1)
    lam_init = jnp.array([0.8 - 0.6 * math.exp(-0.3 * l) for l in range(depth)], F32)
    scal_d = jnp.concatenate(
        [lam_init[:, None], bound_d[:, None] * LOG2E,
         jnp.broadcast_to(slopes_diff * LOG2E, (depth, DIFF_HEADS))], axis=1)

    xf = x.reshape(t, d)
    for l in range(depth):
        qa, ka, va, qd, kd, vd, gates = _proj(xf, gmix, w_in_b, gcol, bd, bg, layer=l, tm=PROJ_ROWS)

        o_a = lax.cond(
            bound_a[l] <= FIXED_REF_MAX_BOUND,
            lambda a: _swa(*a, bsz=bsz, seq=seq, tq=SWA_QUERIES, fixed_reference=True),
            lambda a: _swa(*a, bsz=bsz, seq=seq, tq=SWA_ONLINE_QUERIES, fixed_reference=False),
            (qa, ka, va, scal_a[l]))

        o_b = lax.cond(
            bound_d[l] <= FIXED_REF_MAX_BOUND,
            lambda a: _diff(*a, layer=l, bsz=bsz, seq=seq, tq=DIFF_QUERIES, fixed_reference=True),
            lambda a: _diff(*a, layer=l, bsz=bsz, seq=seq, tq=DIFF_ONLINE_QUERIES,
                            fixed_reference=False),
            (qd, kd, vd, scal_d[l], lp, gsub))

        xf = _mixffn(xf, o_a, o_b, gates, w_branch_b, w_o_b, gffn, w_ffn_in_b, w_ffn_out_b,
                     layer=l, tm=MIXFFN_ROWS, chunk=FFN_CHUNK)
    return xf.reshape(bsz, seq, d)
```
